```python
import math
import jax, jax.numpy as jnp
from jax import lax
import numpy as np

D_MODEL = 1024
BATCH = 2
SEQ = 8192
DEPTH = 4

N_MIXERS = 3
D_FF = 2816
RMS_EPS = 1e-6
S5_WIDTH = D_MODEL
S5_GROUP = 16
S5_GROUPS = S5_WIDTH // S5_GROUP
S5_STATE = 64
S5_CHUNK = 128
S5_DT_MIN = 1e-3
S5_DT_MAX = 1e-1
SB_HEAD_DIM = 64
SB_HEADS = D_MODEL // SB_HEAD_DIM
SB_BLOCK = 128
LRU_WIDTH = D_MODEL
LRU_BLOCK_WIDTH = 256
LRU_BLOCKS = LRU_WIDTH // LRU_BLOCK_WIDTH
LRU_CONV = 4
LRU_C = 8.0
N_S5 = len(range(0, DEPTH, N_MIXERS))
N_SB = len(range(1, DEPTH, N_MIXERS))
N_LRU = len(range(2, DEPTH, N_MIXERS))

kernel_name = "hybrid_s5_stickbreak_rglru_macaron"


def rms_norm(x, g):
    xf = x.astype(jnp.float32)
    y = xf * lax.rsqrt(jnp.mean(xf * xf, axis=-1, keepdims=True) + RMS_EPS)
    return (y * g.astype(jnp.float32)).astype(x.dtype)


def swiglu_ffn(h, w_in, w_out):
    gate, up = jnp.split(h @ w_in, 2, axis=-1)
    return (jax.nn.silu(gate) * up) @ w_out


def _linear_recurrence_op(left, right):
    a_l, b_l = left
    a_r, b_r = right
    return a_r * a_l, a_r * b_l + b_r


def s5_mixer(h, w_in, lam_re, lam_im, log_dt, b_re, b_im, c_re, c_im, d_skip, w_out):
    f32 = jnp.float32
    bsz, seq, _ = h.shape
    u = (h @ w_in).astype(f32)
    lam = lax.complex(lam_re.astype(f32), lam_im.astype(f32))
    lam_dt = lam * jnp.exp(log_dt.astype(f32))[:, None]
    lam_bar = jnp.exp(lam_dt)
    b_bar = ((lam_bar - 1.0) / lam)[:, :, None] * lax.complex(b_re.astype(f32), b_im.astype(f32))
    c = lax.complex(c_re.astype(f32), c_im.astype(f32))
    decay_pow = jnp.exp(lam_dt[None] * jnp.arange(1, S5_CHUNK + 1, dtype=f32)[:, None, None])
    n_chunks = seq // S5_CHUNK
    u_blocks = u.reshape(bsz, n_chunks, S5_CHUNK, S5_GROUPS, S5_GROUP).transpose(1, 0, 2, 3, 4)
    a_elems = jnp.broadcast_to(lam_bar, (bsz, S5_CHUNK, S5_GROUPS, S5_STATE))

    def chunk_step(state, u_blk):
        bu = jnp.einsum('gph,btgh->btgp', b_bar, u_blk.astype(jnp.complex64))
        _, s = lax.associative_scan(_linear_recurrence_op, (a_elems, bu), axis=1)
        s = s + decay_pow[None] * state[:, None]
        y_blk = jnp.einsum('ghp,btgp->btgh', c, s).real
        return s[:, -1], y_blk

    state0 = jnp.zeros((bsz, S5_GROUPS, S5_STATE), jnp.complex64)
    _, y = lax.scan(chunk_step, state0, u_blocks)
    y = y.transpose(1, 0, 2, 3, 4).reshape(bsz, seq, S5_WIDTH) + d_skip.astype(f32) * u
    z = jax.nn.gelu(y).astype(h.dtype)
    val, gate = jnp.split(z @ w_out, 2, axis=-1)
    return val * jax.nn.sigmoid(gate)


def stick_breaking_mixer(h, w_qkv, w_out):
    f32 = jnp.float32
    bsz, seq, _ = h.shape
    qkv = (h @ w_qkv).reshape(bsz, seq, 3, SB_HEADS, SB_HEAD_DIM)
    q = qkv[:, :, 0].astype(f32) * (SB_HEAD_DIM ** -0.5)
    k = qkv[:, :, 1].astype(f32)
    v = qkv[:, :, 2].astype(f32)
    n_blocks = seq // SB_BLOCK
    q_blocks = q.reshape(bsz, n_blocks, SB_BLOCK, SB_HEADS, SB_HEAD_DIM).transpose(1, 0, 2, 3, 4)
    key_pos = jnp.arange(seq)

    def attend(args):
        q_blk, blk = args
        logits = jnp.einsum('bqhd,bkhd->bhqk', q_blk, k)
        query_pos = blk * SB_BLOCK + jnp.arange(SB_BLOCK)
        causal = key_pos[None, :] < query_pos[:, None]
        log_keep = jnp.where(causal, jax.nn.log_sigmoid(-logits), 0.0)
        log_remaining = lax.cumsum(log_keep, axis=3, reverse=True) - log_keep
        weights = jnp.where(causal, jnp.exp(jax.nn.log_sigmoid(logits) + log_remaining), 0.0)
        return jnp.einsum('bhqk,bkhd->bqhd', weights, v)

    o = lax.map(attend, (q_blocks, jnp.arange(n_blocks)))
    o = o.transpose(1, 0, 2, 3, 4).reshape(bsz, seq, D_MODEL).astype(h.dtype)
    return o @ w_out


def rglru_mixer(h, w_in, conv_w, conv_b, w_a, b_a, w_x, b_x, lam, w_out):
    f32 = jnp.float32
    bsz, seq, _ = h.shape
    branch_gelu, branch_rnn = jnp.split(h @ w_in, 2, axis=-1)
    xc = lax.conv_general_dilated(
        branch_rnn, conv_w[:, None, :], window_strides=(1,), padding=[(LRU_CONV - 1, 0)],
        dimension_numbers=('NWC', 'WIO', 'NWC'), feature_group_count=LRU_WIDTH) + conv_b
    xb = xc.reshape(bsz, seq, LRU_BLOCKS, LRU_BLOCK_WIDTH)
    r = jax.nn.sigmoid(jnp.einsum('blnc,ncd->blnd', xb, w_a) + b_a).reshape(bsz, seq, LRU_WIDTH)
    i = jax.nn.sigmoid(jnp.einsum('blnc,ncd->blnd', xb, w_x) + b_x).reshape(bsz, seq, LRU_WIDTH)
    log_a = (-LRU_C * r.astype(f32)) * jax.nn.softplus(-lam.astype(f32))
    a = jnp.exp(log_a)
    gated_x = (i * xc).astype(f32) * jnp.sqrt(-jnp.expm1(2.0 * log_a))
    _, hseq = lax.associative_scan(_linear_recurrence_op, (a, gated_x), axis=1)
    y = (jax.nn.gelu(branch_gelu.astype(f32)) * hseq).astype(h.dtype)
    return y @ w_out


def setup_inputs(seed: int = 0) -> dict:
    key = jax.random.key(seed)
    ks = iter(jax.random.split(key, 40))

    def dense(shape, fan_in):
        return jax.random.normal(next(ks), shape, jnp.float32) * (fan_in ** -0.5)

    def gain(shape):
        return 1.0 + 0.02 * jax.random.normal(next(ks), shape, jnp.float32)

    def small(shape):
        return 0.01 * jax.random.normal(next(ks), shape, jnp.float32)

    x = jax.random.normal(next(ks), (BATCH, SEQ, D_MODEL), jnp.float32)
    inp = {"x": x}
    inp["ffn1_norm"] = gain((DEPTH, D_MODEL))
    inp["ffn1_w_in"] = dense((DEPTH, D_MODEL, 2 * D_FF), D_MODEL)
    inp["ffn1_w_out"] = dense((DEPTH, D_FF, D_MODEL), D_FF)
    inp["mix_norm"] = gain((DEPTH, D_MODEL))
    inp["ffn2_norm"] = gain((DEPTH, D_MODEL))
    inp["ffn2_w_in"] = dense((DEPTH, D_MODEL, 2 * D_FF), D_MODEL)
    inp["ffn2_w_out"] = dense((DEPTH, D_FF, D_MODEL), D_FF)
    inp["final_norm"] = gain((D_MODEL,))
    inp["s5_w_in"] = dense((N_S5, D_MODEL, S5_WIDTH), D_MODEL)
    inp["s5_lam_re"] = -0.5 + small((N_S5, S5_GROUPS, S5_STATE))
    inp["s5_lam_im"] = (math.pi * jnp.arange(S5_STATE, dtype=jnp.float32))[None, None, :] + small((N_S5, S5_GROUPS, S5_STATE))
    inp["s5_log_dt"] = jax.random.uniform(next(ks), (N_S5, S5_GROUPS), jnp.float32,
                                          math.log(S5_DT_MIN), math.log(S5_DT_MAX))
    inp["s5_b_re"] = dense((N_S5, S5_GROUPS, S5_STATE, S5_GROUP), 2 * S5_GROUP)
    inp["s5_b_im"] = dense((N_S5, S5_GROUPS, S5_STATE, S5_GROUP), 2 * S5_GROUP)
    inp["s5_c_re"] = dense((N_S5, S5_GROUPS, S5_GROUP, S5_STATE), S5_STATE)
    inp["s5_c_im"] = dense((N_S5, S5_GROUPS, S5_GROUP, S5_STATE), S5_STATE)
    inp["s5_d"] = jax.random.normal(next(ks), (N_S5, S5_WIDTH), jnp.float32)
    inp["s5_w_out"] = dense((N_S5, S5_WIDTH, 2 * D_MODEL), S5_WIDTH)
    inp["sb_w_qkv"] = dense((N_SB, D_MODEL, 3 * D_MODEL), D_MODEL)
    inp["sb_w_out"] = dense((N_SB, D_MODEL, D_MODEL), D_MODEL)
    inp["lru_w_in"] = dense((N_LRU, D_MODEL, 2 * LRU_WIDTH), D_MODEL)
    inp["lru_conv_w"] = dense((N_LRU, LRU_CONV, LRU_WIDTH), LRU_CONV)
    inp["lru_conv_b"] = small((N_LRU, LRU_WIDTH))
    inp["lru_w_a"] = dense((N_LRU, LRU_BLOCKS, LRU_BLOCK_WIDTH, LRU_BLOCK_WIDTH), LRU_BLOCK_WIDTH)
    inp["lru_b_a"] = small((N_LRU, LRU_BLOCKS, LRU_BLOCK_WIDTH))
    inp["lru_w_x"] = dense((N_LRU, LRU_BLOCKS, LRU_BLOCK_WIDTH, LRU_BLOCK_WIDTH), LRU_BLOCK_WIDTH)
    inp["lru_b_x"] = small((N_LRU, LRU_BLOCKS, LRU_BLOCK_WIDTH))
    a0 = jax.random.uniform(next(ks), (N_LRU, LRU_WIDTH), jnp.float32, 0.9, 0.999)
    p = jnp.exp(jnp.log(a0) / LRU_C)
    inp["lru_lambda"] = jnp.log(p) - jnp.log1p(-p)
    inp["lru_w_out"] = dense((N_LRU, LRU_WIDTH, D_MODEL), LRU_WIDTH)
    return inp


def reference(x, ffn1_norm, ffn1_w_in, ffn1_w_out, mix_norm, ffn2_norm, ffn2_w_in, ffn2_w_out, final_norm,
              s5_w_in, s5_lam_re, s5_lam_im, s5_log_dt, s5_b_re, s5_b_im, s5_c_re, s5_c_im, s5_d, s5_w_out,
              sb_w_qkv, sb_w_out,
              lru_w_in, lru_conv_w, lru_conv_b, lru_w_a, lru_b_a, lru_w_x, lru_b_x, lru_lambda, lru_w_out):
    h = x
    for layer in range(DEPTH):
        h = h + 0.5 * swiglu_ffn(rms_norm(h, ffn1_norm[layer]), ffn1_w_in[layer], ffn1_w_out[layer])
        hn = rms_norm(h, mix_norm[layer])
        kind = layer % N_MIXERS
        j = layer // N_MIXERS
        if kind == 0:
            mixed = s5_mixer(hn, s5_w_in[j], s5_lam_re[j], s5_lam_im[j], s5_log_dt[j], s5_b_re[j], s5_b_im[j],
                             s5_c_re[j], s5_c_im[j], s5_d[j], s5_w_out[j])
        elif kind == 1:
            mixed = stick_breaking_mixer(hn, sb_w_qkv[j], sb_w_out[j])
        else:
            mixed = rglru_mixer(hn, lru_w_in[j], lru_conv_w[j], lru_conv_b[j], lru_w_a[j], lru_b_a[j],
                                lru_w_x[j], lru_b_x[j], lru_lambda[j], lru_w_out[j])
        h = h + mixed
        h = h + 0.5 * swiglu_ffn(rms_norm(h, ffn2_norm[layer]), ffn2_w_in[layer], ffn2_w_out[layer])
    return rms_norm(h, final_norm)
```

```python
import functools
import math

import jax
import jax.numpy as jnp
from jax import lax
from jax.experimental import pallas as pl
from jax.experimental.pallas import tpu as pltpu

F32 = jnp.float32
BF16 = jnp.bfloat16

RMS_EPS = 1e-6
N_MIXERS = 3
S5_GROUP = 16
S5_STATE = 64
SB_HEAD_DIM = 64
LRU_BLOCK_WIDTH = 256
LRU_CONV = 4
LRU_C = 8.0

LANES = 128
SUBLANES = 8
VMEM_LIMIT_BYTES = 56 * 1024 * 1024

FFN_TM = 512
FFN_CHUNK = 256
S5_TM = 256
LRU_TM = 256
PROJ_TM = 512
ATT_TQ = 256
ATT_TK = 256


def _params(*sem):
    return pltpu.CompilerParams(dimension_semantics=sem, vmem_limit_bytes=VMEM_LIMIT_BYTES)


def _resident(shape):
    nd = len(shape)
    return pl.BlockSpec(shape, lambda *_: (0,) * nd, pipeline_mode=pl.Buffered(1))


def _rms_norm(x, g):
    ms = jnp.mean(x * x, axis=-1, keepdims=True)
    return (x * lax.rsqrt(ms + RMS_EPS)) * g


def _gelu_tanh(x):
    c = math.sqrt(2.0 / math.pi)
    return 0.5 * x * (1.0 + jnp.tanh(c * (x + 0.044715 * (x * x * x))))


def _dot(a, b):
    return jnp.dot(a, b, preferred_element_type=F32)


def _ffn_kernel(x_ref, g_ref, win_ref, wout_ref, o_ref, *, d_ff, chunk):
    x = x_ref[...]
    xn = _rms_norm(x, g_ref[...]).astype(BF16)
    acc = jnp.zeros(x.shape, F32)
    for c in range(d_ff // chunk):
        gate = _dot(xn, win_ref[:, c * chunk:(c + 1) * chunk])
        up = _dot(xn, win_ref[:, d_ff + c * chunk:d_ff + (c + 1) * chunk])
        act = (gate * jax.nn.sigmoid(gate) * up).astype(BF16)
        acc = acc + _dot(act, wout_ref[c * chunk:(c + 1) * chunk, :])
    o_ref[...] = x + 0.5 * acc


def _ffn(h, g, w_in, w_out):
    t, d = h.shape
    d_ff = w_out.shape[0]
    tm = FFN_TM
    return pl.pallas_call(
        functools.partial(_ffn_kernel, d_ff=d_ff, chunk=FFN_CHUNK),
        grid=(t // tm,),
        in_specs=[
            pl.BlockSpec((tm, d), lambda i: (i, 0)),
            _resident((1, d)),
            _resident((d, 2 * d_ff)),
            _resident((d_ff, d)),
        ],
        out_specs=pl.BlockSpec((tm, d), lambda i: (i, 0)),
        out_shape=jax.ShapeDtypeStruct((t, d), F32),
        compiler_params=_params("parallel"),
        name="ffn",
    )(h, g.reshape(1, d), w_in, w_out)


def _norm_kernel(x_ref, g_ref, o_ref):
    o_ref[...] = _rms_norm(x_ref[...], g_ref[...])


def _final_norm(h, g):
    t, d = h.shape
    tm = PROJ_TM
    return pl.pallas_call(
        _norm_kernel,
        grid=(t // tm,),
        in_specs=[pl.BlockSpec((tm, d), lambda i: (i, 0)), _resident((1, d))],
        out_specs=pl.BlockSpec((tm, d), lambda i: (i, 0)),
        out_shape=jax.ShapeDtypeStruct((t, d), F32),
        compiler_params=_params("parallel"),
        name="final_norm",
    )(h, g.reshape(1, d))


def _s5_kernel(x_ref, g_ref, win_ref, bmat_ref, cmat_ref, are_ref, aim_ref, dskip_ref, wout_ref,
               o_ref, xre_ref, xim_ref, sre_ref, sim_ref, *, tm, n_pairs):
    d = x_ref.shape[-1]

    @pl.when(pl.program_id(1) == 0)
    def _():
        sre_ref[...] = jnp.zeros(sre_ref.shape, F32)
        sim_ref[...] = jnp.zeros(sim_ref.shape, F32)

    x = x_ref[0]
    hn = _rms_norm(x, g_ref[...]).astype(BF16)
    u = _dot(hn, win_ref[...])
    ub = u.astype(BF16)

    for o in range(n_pairs):
        lhs = jnp.concatenate(
            [ub[:, LANES * o:LANES * (o + 1)], ub[:, LANES * (o + n_pairs):LANES * (o + n_pairs + 1)]], axis=1)
        bu = _dot(lhs, bmat_ref[o])
        for j in range(SUBLANES):
            xre_ref[o, pl.ds(j, tm, stride=SUBLANES), :] = bu[:, 2 * LANES * j:2 * LANES * j + LANES]
            xim_ref[o, pl.ds(j, tm, stride=SUBLANES), :] = bu[:, 2 * LANES * j + LANES:2 * LANES * (j + 1)]

    a_re = [are_ref[o] for o in range(n_pairs)]
    a_im = [aim_ref[o] for o in range(n_pairs)]

    def step(t, carry):
        row = pl.multiple_of(t * SUBLANES, SUBLANES)
        new = []
        for o in range(n_pairs):
            s_re, s_im = carry[2 * o], carry[2 * o + 1]
            n_re = a_re[o] * s_re - a_im[o] * s_im + xre_ref[o, pl.ds(row, SUBLANES), :]
            n_im = a_re[o] * s_im + a_im[o] * s_re + xim_ref[o, pl.ds(row, SUBLANES), :]
            xre_ref[o, pl.ds(row, SUBLANES), :] = n_re
            xim_ref[o, pl.ds(row, SUBLANES), :] = n_im
            new += [n_re, n_im]
        return tuple(new)

    init = []
    for o in range(n_pairs):
        init += [sre_ref[o], sim_ref[o]]
    final = lax.fori_loop(0, tm, step, tuple(init), unroll=4)
    for o in range(n_pairs):
        sre_ref[o] = final[2 * o]
        sim_ref[o] = final[2 * o + 1]

    y_tiles = [None] * (2 * n_pairs)
    for o in range(n_pairs):
        pieces = []
        for j in range(SUBLANES):
            pieces.append(xre_ref[o, pl.ds(j, tm, stride=SUBLANES), :].astype(BF16))
            pieces.append(xim_ref[o, pl.ds(j, tm, stride=SUBLANES), :].astype(BF16))
        y_o = _dot(jnp.concatenate(pieces, axis=1), cmat_ref[o])
        y_tiles[o] = y_o[:, :LANES]
        y_tiles[o + n_pairs] = y_o[:, LANES:]
    y = jnp.concatenate(y_tiles, axis=1) + dskip_ref[...] * u
    z = _gelu_tanh(y).astype(BF16)
    vg = _dot(z, wout_ref[...])
    o_ref[0] = x + vg[:, :d] * jax.nn.sigmoid(vg[:, d:])


def _s5_operands(lam_re, lam_im, log_dt, b_re, b_im, c_re, c_im):
    g, p = lam_re.shape
    hch = b_re.shape[-1]
    n_pairs = g // (2 * SUBLANES)
    lam = lax.complex(lam_re.astype(F32), lam_im.astype(F32))
    lam_dt = lam * jnp.exp(log_dt.astype(F32))[:, None]
    lam_bar = jnp.exp(lam_dt)
    b_bar = ((lam_bar - 1.0) / lam)[:, :, None] * lax.complex(b_re.astype(F32), b_im.astype(F32))
    eye_j = jnp.eye(SUBLANES, dtype=F32)
    eye_a = jnp.eye(2, dtype=F32)
    bparts = jnp.stack([b_bar.real, b_bar.imag]).reshape(2, 2, n_pairs, SUBLANES, p, hch)
    bmat = jnp.einsum('qaojph,jk,ab->oajhkqbp', bparts, eye_j, eye_a)
    bmat = bmat.reshape(n_pairs, 2 * SUBLANES * hch, SUBLANES * 2 * 2 * p)
    cparts = jnp.stack([c_re.astype(F32), -c_im.astype(F32)]).reshape(2, 2, n_pairs, SUBLANES, hch, p)
    cmat = jnp.einsum('qaojhp,jk,ab->ojqapbkh', cparts, eye_j, eye_a)
    cmat = cmat.reshape(n_pairs, SUBLANES * 2 * 2 * p, 2 * SUBLANES * hch)

    def lanes(v):
        return v.reshape(2, n_pairs, SUBLANES, p).transpose(1, 2, 0, 3).reshape(n_pairs, SUBLANES, 2 * p)

    return bmat.astype(BF16), cmat.astype(BF16), lanes(lam_bar.real), lanes(lam_bar.imag)


def _s5_mixer(h, g, w_in, lam_re, lam_im, log_dt, b_re, b_im, c_re, c_im, d_skip, w_out, bsz):
    t, d = h.shape
    seq = t // bsz
    tm = S5_TM
    bmat, cmat, a_re, a_im = _s5_operands(lam_re, lam_im, log_dt, b_re, b_im, c_re, c_im)
    n_pairs = bmat.shape[0]
    assert 2 * n_pairs * LANES == d and 2 * S5_STATE == LANES
    out = pl.pallas_call(
        functools.partial(_s5_kernel, tm=tm, n_pairs=n_pairs),
        grid=(bsz, seq // tm),
        in_specs=[
            pl.BlockSpec((1, tm, d), lambda b, i: (b, i, 0)),
            _resident((1, d)),
            _resident(w_in.shape),
            _resident(bmat.shape),
            _resident(cmat.shape),
            _resident(a_re.shape),
            _resident(a_im.shape),
            _resident((1, d)),
            _resident(w_out.shape),
        ],
        out_specs=pl.BlockSpec((1, tm, d), lambda b, i: (b, i, 0)),
        out_shape=jax.ShapeDtypeStruct((bsz, seq, d), F32),
        scratch_shapes=[
            pltpu.VMEM((n_pairs, tm * SUBLANES, LANES), F32),
            pltpu.VMEM((n_pairs, tm * SUBLANES, LANES), F32),
            pltpu.VMEM((n_pairs, SUBLANES, LANES), F32),
            pltpu.VMEM((n_pairs, SUBLANES, LANES), F32),
        ],
        compiler_params=_params("arbitrary", "arbitrary"),
        name="s5_mixer",
    )(h.reshape(bsz, seq, d), g.reshape(1, d), w_in, bmat, cmat, a_re, a_im, d_skip.reshape(1, d), w_out)
    return out.reshape(t, d)


def _lru_kernel(x_ref, g_ref, win_ref, cw_ref, cb_ref, wa_ref, ba_ref, wx_ref, bx_ref, lam_ref, wout_ref,
                o_ref, br_ref, a_ref, gx_ref, h_ref, *, tm, n_blocks):
    d = x_ref.shape[-1]
    n_tiles = d // LANES
    first = pl.program_id(1) == 0

    @pl.when(first)
    def _():
        br_ref[0:SUBLANES, :] = jnp.zeros((SUBLANES, d), F32)
        h_ref[...] = jnp.zeros(h_ref.shape, F32)

    x = x_ref[0]
    hn = _rms_norm(x, g_ref[...]).astype(BF16)
    both = _dot(hn, win_ref[...])
    branch_gelu = both[:, :d]
    br_ref[SUBLANES:SUBLANES + tm, :] = both[:, d:]
    xc = cb_ref[...]
    for k in range(LRU_CONV):
        off = SUBLANES - (LRU_CONV - 1) + k
        xc = xc + cw_ref[k:k + 1, :] * br_ref[off:off + tm, :]
    br_ref[0:SUBLANES, :] = br_ref[tm:tm + SUBLANES, :]

    xcb = xc.astype(BF16)
    bw = LRU_BLOCK_WIDTH
    r_parts, i_parts = [], []
    for n in range(n_blocks):
        xs = xcb[:, n * bw:(n + 1) * bw]
        r_parts.append(jax.nn.sigmoid(_dot(xs, wa_ref[n]) + ba_ref[:, n * bw:(n + 1) * bw]))
        i_parts.append(jax.nn.sigmoid(_dot(xs, wx_ref[n]) + bx_ref[:, n * bw:(n + 1) * bw]))
    r = jnp.concatenate(r_parts, axis=1)
    gate_i = jnp.concatenate(i_parts, axis=1)
    lam = lam_ref[...]
    softplus_neg_lam = jnp.maximum(-lam, 0.0) + jnp.log1p(jnp.exp(-jnp.abs(lam)))
    log_a = (-LRU_C * r) * softplus_neg_lam
    a = jnp.exp(log_a)
    gated = (gate_i * xc) * jnp.sqrt(-jnp.tanh(log_a) * (a * a + 1.0))

    for c in range(n_tiles):
        a_ref[pl.ds(c, tm, stride=SUBLANES), :] = a[:, c * LANES:(c + 1) * LANES]
        gx_ref[pl.ds(c, tm, stride=SUBLANES), :] = gated[:, c * LANES:(c + 1) * LANES]

    def step(t, hprev):
        row = pl.multiple_of(t * SUBLANES, SUBLANES)
        hnew = a_ref[pl.ds(row, SUBLANES), :] * hprev + gx_ref[pl.ds(row, SUBLANES), :]
        gx_ref[pl.ds(row, SUBLANES), :] = hnew
        return hnew

    h_ref[...] = lax.fori_loop(0, tm, step, h_ref[...], unroll=8)
    hseq = jnp.concatenate([gx_ref[pl.ds(c, tm, stride=SUBLANES), :] for c in range(n_tiles)], axis=1)
    y = (_gelu_tanh(branch_gelu) * hseq).astype(BF16)
    o_ref[0] = x + _dot(y, wout_ref[...])


def _lru_mixer(h, g, w_in, conv_w, conv_b, w_a, b_a, w_x, b_x, lam, w_out, bsz):
    t, d = h.shape
    seq = t // bsz
    tm = LRU_TM
    n_blocks = w_a.shape[0]
    assert d == SUBLANES * LANES and n_blocks * LRU_BLOCK_WIDTH == d
    out = pl.pallas_call(
        functools.partial(_lru_kernel, tm=tm, n_blocks=n_blocks),
        grid=(bsz, seq // tm),
        in_specs=[
            pl.BlockSpec((1, tm, d), lambda b, i: (b, i, 0)),
            _resident((1, d)),
            _resident(w_in.shape),
            _resident(conv_w.shape),
            _resident((1, d)),
            _resident(w_a.shape),
            _resident((1, d)),
            _resident(w_x.shape),
            _resident((1, d)),
            _resident((1, d)),
            _resident(w_out.shape),
        ],
        out_specs=pl.BlockSpec((1, tm, d), lambda b, i: (b, i, 0)),
        out_shape=jax.ShapeDtypeStruct((bsz, seq, d), F32),
        scratch_shapes=[
            pltpu.VMEM((tm + SUBLANES, d), F32),
            pltpu.VMEM((tm * SUBLANES, LANES), F32),
            pltpu.VMEM((tm * SUBLANES, LANES), F32),
            pltpu.VMEM((SUBLANES, LANES), F32),
        ],
        compiler_params=_params("arbitrary", "arbitrary"),
        name="lru_mixer",
    )(h.reshape(bsz, seq, d), g.reshape(1, d), w_in, conv_w, conv_b.reshape(1, d), w_a, b_a.reshape(1, d),
      w_x, b_x.reshape(1, d), lam.reshape(1, d), w_out)
    return out.reshape(t, d)


def _qkv_kernel(x_ref, g_ref, w_ref, q_ref, k_ref, v_ref, *, n_heads):
    d = x_ref.shape[-1]
    hd = SB_HEAD_DIM
    xn = _rms_norm(x_ref[0], g_ref[...]).astype(BF16)
    qkv = _dot(xn, w_ref[...])
    scale = hd ** -0.5
    for hh in range(n_heads):
        q_ref[0, hh] = (qkv[:, hh * hd:(hh + 1) * hd] * scale).astype(BF16)
        k_ref[0, hh] = qkv[:, d + hh * hd:d + (hh + 1) * hd].astype(BF16)
        v_ref[0, hh] = qkv[:, 2 * d + hh * hd:2 * d + (hh + 1) * hd].astype(BF16)


def _qkv_proj(h, g, w_qkv, bsz):
    t, d = h.shape
    seq = t // bsz
    hd = SB_HEAD_DIM
    n_heads = d // hd
    tm = PROJ_TM
    head_major = jax.ShapeDtypeStruct((bsz, n_heads, seq, hd), BF16)
    out_spec = pl.BlockSpec((1, n_heads, tm, hd), lambda b, i: (b, 0, i, 0))
    return pl.pallas_call(
        functools.partial(_qkv_kernel, n_heads=n_heads),
        grid=(bsz, seq // tm),
        in_specs=[
            pl.BlockSpec((1, tm, d), lambda b, i: (b, i, 0)),
            _resident((1, d)),
            _resident(w_qkv.shape),
        ],
        out_specs=[out_spec, out_spec, out_spec],
        out_shape=[head_major, head_major, head_major],
        compiler_params=_params("parallel", "parallel"),
        name="sb_qkv",
    )(h.reshape(bsz, seq, d), g.reshape(1, d), w_qkv)


def _att_kernel(q_ref, k_ref, v_ref, o_ref, *, tq, tk, heads_per_step):
    i = pl.program_id(2)
    rows = lax.broadcasted_iota(jnp.int32, (tk, tk), 0)
    cols = lax.broadcasted_iota(jnp.int32, (tk, tk), 1)
    tri = (rows > cols).astype(BF16)
    causal = lax.broadcasted_iota(jnp.int32, (tq, tk), 1) < lax.broadcasted_iota(jnp.int32, (tq, tk), 0)

    def block(q, kb, vb, carry, acc, masked):
        z = lax.dot_general(q, kb, (((1,), (1,)), ((), ())), preferred_element_type=F32)
        softplus = jnp.maximum(z, 0.0) + jnp.log(1.0 + jnp.exp(-jnp.abs(z)))
        sp = jnp.where(causal, softplus, 0.0) if masked else softplus
        sp_hi = sp.astype(BF16)
        sp_lo = (sp - sp_hi.astype(F32)).astype(BF16)
        right = _dot(sp_hi, tri) + _dot(sp_lo, tri)
        w = jnp.exp((z - softplus) - right - carry)
        if masked:
            w = jnp.where(causal, w, 0.0)
        acc = acc + _dot(w.astype(BF16), vb)
        carry = carry + right[:, 0:1] + sp[:, 0:1]
        return carry, acc

    outs = []
    for hh in range(heads_per_step):
        q = q_ref[0, hh]
        start = pl.multiple_of(i * tk, tk)
        carry0 = jnp.zeros((tq, 1), F32)
        acc0 = jnp.zeros((tq, SB_HEAD_DIM), F32)
        carry, acc = block(q, k_ref[0, hh, pl.ds(start, tk), :], v_ref[0, hh, pl.ds(start, tk), :],
                           carry0, acc0, True)

        def body(n, ca, hh=hh, q=q):
            j = i - 1 - n
            s0 = pl.multiple_of(j * tk, tk)
            return block(q, k_ref[0, hh, pl.ds(s0, tk), :], v_ref[0, hh, pl.ds(s0, tk), :], ca[0], ca[1], False)

        carry, acc = lax.fori_loop(0, i, body, (carry, acc))
        outs.append(acc)
    o_ref[0] = jnp.concatenate(outs, axis=1).astype(o_ref.dtype)


def _attention(q, k, v):
    bsz, n_heads, seq, hd = q.shape
    tq, tk = ATT_TQ, ATT_TK
    hps = LANES // hd
    return pl.pallas_call(
        functools.partial(_att_kernel, tq=tq, tk=tk, heads_per_step=hps),
        grid=(bsz, n_heads // hps, seq // tq),
        in_specs=[
            pl.BlockSpec((1, hps, tq, hd), lambda b, h, i: (b, h, i, 0)),
            pl.BlockSpec((1, hps, seq, hd), lambda b, h, i: (b, h, 0, 0)),
            pl.BlockSpec((1, hps, seq, hd), lambda b, h, i: (b, h, 0, 0)),
        ],
        out_specs=pl.BlockSpec((1, tq, hps * hd), lambda b, h, i: (b, i, h)),
        out_shape=jax.ShapeDtypeStruct((bsz, seq, n_heads * hd), BF16),
        compiler_params=_params("parallel", "parallel", "arbitrary"),
        name="sb_attention",
    )(q, k, v)


def _proj_kernel(a_ref, w_ref, x_ref, o_ref):
    o_ref[...] = x_ref[...] + _dot(a_ref[...], w_ref[...])


def _out_proj(a, w, h):
    t, d = h.shape
    tm = PROJ_TM
    return pl.pallas_call(
        _proj_kernel,
        grid=(t // tm,),
        in_specs=[
            pl.BlockSpec((tm, a.shape[1]), lambda i: (i, 0)),
            _resident(w.shape),
            pl.BlockSpec((tm, d), lambda i: (i, 0)),
        ],
        out_specs=pl.BlockSpec((tm, d), lambda i: (i, 0)),
        out_shape=jax.ShapeDtypeStruct((t, d), F32),
        compiler_params=_params("parallel"),
        name="sb_out_proj",
    )(a, w, h)


def _sb_mixer(h, g, w_qkv, w_out, bsz):
    t, d = h.shape
    q, k, v = _qkv_proj(h, g, w_qkv, bsz)
    o = _attention(q, k, v)
    return _out_proj(o.reshape(t, d), w_out, h)


def kernel(x, ffn1_norm, ffn1_w_in, ffn1_w_out, mix_norm, ffn2_norm, ffn2_w_in, ffn2_w_out, final_norm,
           s5_w_in, s5_lam_re, s5_lam_im, s5_log_dt, s5_b_re, s5_b_im, s5_c_re, s5_c_im, s5_d, s5_w_out,
           sb_w_qkv, sb_w_out,
           lru_w_in, lru_conv_w, lru_conv_b, lru_w_a, lru_b_a, lru_w_x, lru_b_x, lru_lambda, lru_w_out):
    bsz, seq, d = x.shape
    depth = ffn1_norm.shape[0]
    h = x.reshape(bsz * seq, d)
    bf = lambda w: w.astype(BF16)
    for layer in range(depth):
        h = _ffn(h, ffn1_norm[layer], bf(ffn1_w_in[layer]), bf(ffn1_w_out[layer]))
        kind, j = layer % N_MIXERS, layer // N_MIXERS
        if kind == 0:
            h = _s5_mixer(h, mix_norm[layer], bf(s5_w_in[j]), s5_lam_re[j], s5_lam_im[j], s5_log_dt[j],
                          s5_b_re[j], s5_b_im[j], s5_c_re[j], s5_c_im[j], s5_d[j], bf(s5_w_out[j]), bsz)
        elif kind == 1:
            h = _sb_mixer(h, mix_norm[layer], bf(sb_w_qkv[j]), bf(sb_w_out[j]), bsz)
        else:
            h = _lru_mixer(h, mix_norm[layer], bf(lru_w_in[j]), lru_conv_w[j], lru_conv_b[j], bf(lru_w_a[j]),
                           lru_b_a[j], bf(lru_w_x[j]), lru_b_x[j], lru_lambda[j], bf(lru_w_out[j]), bsz)
        h = _ffn(h, ffn2_norm[layer], bf(ffn2_w_in[layer]), bf(ffn2_w_out[layer]))
    return _final_norm(h, final_norm).reshape(bsz, seq, d)
```

```python
import functools
import math

import jax
import jax.numpy as jnp
from jax import lax
from jax.experimental import pallas as pl
from jax.experimental.pallas import tpu as pltpu

F32 = jnp.float32
BF16 = jnp.bfloat16

RMS_EPS = 1e-6
N_MIXERS = 3
S5_GROUP = 16
S5_STATE = 64
SB_HEAD_DIM = 64
LRU_BLOCK_WIDTH = 256
LRU_CONV = 4
LRU_C = 8.0

LANES = 128
SUBLANES = 8
VMEM_LIMIT_BYTES = 56 * 1024 * 1024

FFN_TM = 512
FFN_CHUNK = 256
S5_TM = 256
LRU_TM = 256
PROJ_TM = 512
ATT_TQ = 256
ATT_TK = 256
ATT_LANES = 512


def _params(*sem):
    return pltpu.CompilerParams(dimension_semantics=sem, vmem_limit_bytes=VMEM_LIMIT_BYTES)


def _resident(shape):
    nd = len(shape)
    return pl.BlockSpec(shape, lambda *_: (0,) * nd, pipeline_mode=pl.Buffered(1))


def _rms_norm(x, g):
    ms = jnp.mean(x * x, axis=-1, keepdims=True)
    return (x * lax.rsqrt(ms + RMS_EPS)) * g


def _gelu_tanh(x):
    c = math.sqrt(2.0 / math.pi)
    return 0.5 * x * (1.0 + jnp.tanh(c * (x + 0.044715 * (x * x * x))))


def _dot(a, b):
    return jnp.dot(a, b, preferred_element_type=F32)


def _ffn_kernel(x_ref, g_ref, win_ref, wout_ref, o_ref, *, d_ff, chunk):
    x = x_ref[...]
    xn = _rms_norm(x, g_ref[...]).astype(BF16)
    acc = jnp.zeros(x.shape, F32)
    for c in range(d_ff // chunk):
        gate = _dot(xn, win_ref[:, c * chunk:(c + 1) * chunk])
        up = _dot(xn, win_ref[:, d_ff + c * chunk:d_ff + (c + 1) * chunk])
        act = (gate * jax.nn.sigmoid(gate) * up).astype(BF16)
        acc = acc + _dot(act, wout_ref[c * chunk:(c + 1) * chunk, :])
    o_ref[...] = x + 0.5 * acc


def _ffn(h, g, w_in, w_out):
    t, d = h.shape
    d_ff = w_out.shape[0]
    tm = FFN_TM
    return pl.pallas_call(
        functools.partial(_ffn_kernel, d_ff=d_ff, chunk=FFN_CHUNK),
        grid=(t // tm,),
        in_specs=[
            pl.BlockSpec((tm, d), lambda i: (i, 0)),
            _resident((1, d)),
            _resident((d, 2 * d_ff)),
            _resident((d_ff, d)),
        ],
        out_specs=pl.BlockSpec((tm, d), lambda i: (i, 0)),
        out_shape=jax.ShapeDtypeStruct((t, d), F32),
        compiler_params=_params("parallel"),
        name="ffn",
    )(h, g.reshape(1, d), w_in, w_out)


def _norm_kernel(x_ref, g_ref, o_ref):
    o_ref[...] = _rms_norm(x_ref[...], g_ref[...])


def _final_norm(h, g):
    t, d = h.shape
    tm = PROJ_TM
    return pl.pallas_call(
        _norm_kernel,
        grid=(t // tm,),
        in_specs=[pl.BlockSpec((tm, d), lambda i: (i, 0)), _resident((1, d))],
        out_specs=pl.BlockSpec((tm, d), lambda i: (i, 0)),
        out_shape=jax.ShapeDtypeStruct((t, d), F32),
        compiler_params=_params("parallel"),
        name="final_norm",
    )(h, g.reshape(1, d))


def _s5_kernel(x_ref, g_ref, win_ref, bmat_ref, cmat_ref, are_ref, aim_ref, dskip_ref, wout_ref,
               o_ref, xre_ref, xim_ref, sre_ref, sim_ref, *, tm, n_pairs):
    d = x_ref.shape[-1]

    @pl.when(pl.program_id(1) == 0)
    def _():
        sre_ref[...] = jnp.zeros(sre_ref.shape, F32)
        sim_ref[...] = jnp.zeros(sim_ref.shape, F32)

    x = x_ref[0]
    hn = _rms_norm(x, g_ref[...]).astype(BF16)
    u = _dot(hn, win_ref[...])
    ub = u.astype(BF16)

    for o in range(n_pairs):
        lhs = jnp.concatenate(
            [ub[:, LANES * o:LANES * (o + 1)], ub[:, LANES * (o + n_pairs):LANES * (o + n_pairs + 1)]], axis=1)
        bu = _dot(lhs, bmat_ref[o])
        for j in range(SUBLANES):
            xre_ref[o, pl.ds(j, tm, stride=SUBLANES), :] = bu[:, 2 * LANES * j:2 * LANES * j + LANES]
            xim_ref[o, pl.ds(j, tm, stride=SUBLANES), :] = bu[:, 2 * LANES * j + LANES:2 * LANES * (j + 1)]

    a_re = [are_ref[o] for o in range(n_pairs)]
    a_im = [aim_ref[o] for o in range(n_pairs)]

    def step(t, carry):
        row = pl.multiple_of(t * SUBLANES, SUBLANES)
        new = []
        for o in range(n_pairs):
            s_re, s_im = carry[2 * o], carry[2 * o + 1]
            n_re = a_re[o] * s_re - a_im[o] * s_im + xre_ref[o, pl.ds(row, SUBLANES), :]
            n_im = a_re[o] * s_im + a_im[o] * s_re + xim_ref[o, pl.ds(row, SUBLANES), :]
            xre_ref[o, pl.ds(row, SUBLANES), :] = n_re
            xim_ref[o, pl.ds(row, SUBLANES), :] = n_im
            new += [n_re, n_im]
        return tuple(new)

    init = []
    for o in range(n_pairs):
        init += [sre_ref[o], sim_ref[o]]
    final = lax.fori_loop(0, tm, step, tuple(init), unroll=4)
    for o in range(n_pairs):
        sre_ref[o] = final[2 * o]
        sim_ref[o] = final[2 * o + 1]

    y_tiles = [None] * (2 * n_pairs)
    for o in range(n_pairs):
        pieces = []
        for j in range(SUBLANES):
            pieces.append(xre_ref[o, pl.ds(j, tm, stride=SUBLANES), :].astype(BF16))
            pieces.append(xim_ref[o, pl.ds(j, tm, stride=SUBLANES), :].astype(BF16))
        y_o = _dot(jnp.concatenate(pieces, axis=1), cmat_ref[o])
        y_tiles[o] = y_o[:, :LANES]
        y_tiles[o + n_pairs] = y_o[:, LANES:]
    y = jnp.concatenate(y_tiles, axis=1) + dskip_ref[...] * u
    z = _gelu_tanh(y).astype(BF16)
    vg = _dot(z, wout_ref[...])
    o_ref[0] = x + vg[:, :d] * jax.nn.sigmoid(vg[:, d:])


def _s5_operands(lam_re, lam_im, log_dt, b_re, b_im, c_re, c_im):
    g, p = lam_re.shape
    hch = b_re.shape[-1]
    n_pairs = g // (2 * SUBLANES)
    lam = lax.complex(lam_re.astype(F32), lam_im.astype(F32))
    lam_dt = lam * jnp.exp(log_dt.astype(F32))[:, None]
    lam_bar = jnp.exp(lam_dt)
    b_bar = ((lam_bar - 1.0) / lam)[:, :, None] * lax.complex(b_re.astype(F32), b_im.astype(F32))
    eye_j = jnp.eye(SUBLANES, dtype=F32)
    eye_a = jnp.eye(2, dtype=F32)
    bparts = jnp.stack([b_bar.real, b_bar.imag]).reshape(2, 2, n_pairs, SUBLANES, p, hch)
    bmat = jnp.einsum('qaojph,jk,ab->oajhkqbp', bparts, eye_j, eye_a)
    bmat = bmat.reshape(n_pairs, 2 * SUBLANES * hch, SUBLANES * 2 * 2 * p)
    cparts = jnp.stack([c_re.astype(F32), -c_im.astype(F32)]).reshape(2, 2, n_pairs, SUBLANES, hch, p)
    cmat = jnp.einsum('qaojhp,jk,ab->ojqapbkh', cparts, eye_j, eye_a)
    cmat = cmat.reshape(n_pairs, SUBLANES * 2 * 2 * p, 2 * SUBLANES * hch)

    def lanes(v):
        return v.reshape(2, n_pairs, SUBLANES, p).transpose(1, 2, 0, 3).reshape(n_pairs, SUBLANES, 2 * p)

    return bmat.astype(BF16), cmat.astype(BF16), lanes(lam_bar.real), lanes(lam_bar.imag)


def _s5_mixer(h, g, w_in, lam_re, lam_im, log_dt, b_re, b_im, c_re, c_im, d_skip, w_out, bsz):
    t, d = h.shape
    seq = t // bsz
    tm = S5_TM
    bmat, cmat, a_re, a_im = _s5_operands(lam_re, lam_im, log_dt, b_re, b_im, c_re, c_im)
    n_pairs = bmat.shape[0]
    assert 2 * n_pairs * LANES == d and 2 * S5_STATE == LANES
    out = pl.pallas_call(
        functools.partial(_s5_kernel, tm=tm, n_pairs=n_pairs),
        grid=(bsz, seq // tm),
        in_specs=[
            pl.BlockSpec((1, tm, d), lambda b, i: (b, i, 0)),
            _resident((1, d)),
            _resident(w_in.shape),
            _resident(bmat.shape),
            _resident(cmat.shape),
            _resident(a_re.shape),
            _resident(a_im.shape),
            _resident((1, d)),
            _resident(w_out.shape),
        ],
        out_specs=pl.BlockSpec((1, tm, d), lambda b, i: (b, i, 0)),
        out_shape=jax.ShapeDtypeStruct((bsz, seq, d), F32),
        scratch_shapes=[
            pltpu.VMEM((n_pairs, tm * SUBLANES, LANES), F32),
            pltpu.VMEM((n_pairs, tm * SUBLANES, LANES), F32),
            pltpu.VMEM((n_pairs, SUBLANES, LANES), F32),
            pltpu.VMEM((n_pairs, SUBLANES, LANES), F32),
        ],
        compiler_params=_params("arbitrary", "arbitrary"),
        name="s5_mixer",
    )(h.reshape(bsz, seq, d), g.reshape(1, d), w_in, bmat, cmat, a_re, a_im, d_skip.reshape(1, d), w_out)
    return out.reshape(t, d)


def _lru_kernel(x_ref, g_ref, win_ref, cw_ref, cb_ref, wa_ref, ba_ref, wx_ref, bx_ref, lam_ref, wout_ref,
                o_ref, br_ref, a_ref, gx_ref, h_ref, *, tm, n_blocks):
    d = x_ref.shape[-1]
    n_tiles = d // LANES
    first = pl.program_id(1) == 0

    @pl.when(first)
    def _():
        br_ref[0:SUBLANES, :] = jnp.zeros((SUBLANES, d), F32)
        h_ref[...] = jnp.zeros(h_ref.shape, F32)

    x = x_ref[0]
    hn = _rms_norm(x, g_ref[...]).astype(BF16)
    both = _dot(hn, win_ref[...])
    branch_gelu = both[:, :d]
    br_ref[SUBLANES:SUBLANES + tm, :] = both[:, d:]
    xc = cb_ref[...]
    for k in range(LRU_CONV):
        off = SUBLANES - (LRU_CONV - 1) + k
        xc = xc + cw_ref[k:k + 1, :] * br_ref[off:off + tm, :]
    br_ref[0:SUBLANES, :] = br_ref[tm:tm + SUBLANES, :]

    xcb = xc.astype(BF16)
    bw = LRU_BLOCK_WIDTH
    r_parts, i_parts = [], []
    for n in range(n_blocks):
        xs = xcb[:, n * bw:(n + 1) * bw]
        r_parts.append(jax.nn.sigmoid(_dot(xs, wa_ref[n]) + ba_ref[:, n * bw:(n + 1) * bw]))
        i_parts.append(jax.nn.sigmoid(_dot(xs, wx_ref[n]) + bx_ref[:, n * bw:(n + 1) * bw]))
    r = jnp.concatenate(r_parts, axis=1)
    gate_i = jnp.concatenate(i_parts, axis=1)
    lam = lam_ref[...]
    softplus_neg_lam = jnp.maximum(-lam, 0.0) + jnp.log1p(jnp.exp(-jnp.abs(lam)))
    log_a = (-LRU_C * r) * softplus_neg_lam
    a = jnp.exp(log_a)
    gated = (gate_i * xc) * jnp.sqrt(-jnp.tanh(log_a) * (a * a + 1.0))

    for c in range(n_tiles):
        a_ref[pl.ds(c, tm, stride=SUBLANES), :] = a[:, c * LANES:(c + 1) * LANES]
        gx_ref[pl.ds(c, tm, stride=SUBLANES), :] = gated[:, c * LANES:(c + 1) * LANES]

    def step(t, hprev):
        row = pl.multiple_of(t * SUBLANES, SUBLANES)
        hnew = a_ref[pl.ds(row, SUBLANES), :] * hprev + gx_ref[pl.ds(row, SUBLANES), :]
        gx_ref[pl.ds(row, SUBLANES), :] = hnew
        return hnew

    h_ref[...] = lax.fori_loop(0, tm, step, h_ref[...], unroll=8)
    hseq = jnp.concatenate([gx_ref[pl.ds(c, tm, stride=SUBLANES), :] for c in range(n_tiles)], axis=1)
    y = (_gelu_tanh(branch_gelu) * hseq).astype(BF16)
    o_ref[0] = x + _dot(y, wout_ref[...])


def _lru_mixer(h, g, w_in, conv_w, conv_b, w_a, b_a, w_x, b_x, lam, w_out, bsz):
    t, d = h.shape
    seq = t // bsz
    tm = LRU_TM
    n_blocks = w_a.shape[0]
    assert d == SUBLANES * LANES and n_blocks * LRU_BLOCK_WIDTH == d
    out = pl.pallas_call(
        functools.partial(_lru_kernel, tm=tm, n_blocks=n_blocks),
        grid=(bsz, seq // tm),
        in_specs=[
            pl.BlockSpec((1, tm, d), lambda b, i: (b, i, 0)),
            _resident((1, d)),
            _resident(w_in.shape),
            _resident(conv_w.shape),
            _resident((1, d)),
            _resident(w_a.shape),
            _resident((1, d)),
            _resident(w_x.shape),
            _resident((1, d)),
            _resident((1, d)),
            _resident(w_out.shape),
        ],
        out_specs=pl.BlockSpec((1, tm, d), lambda b, i: (b, i, 0)),
        out_shape=jax.ShapeDtypeStruct((bsz, seq, d), F32),
        scratch_shapes=[
            pltpu.VMEM((tm + SUBLANES, d), F32),
            pltpu.VMEM((tm * SUBLANES, LANES), F32),
            pltpu.VMEM((tm * SUBLANES, LANES), F32),
            pltpu.VMEM((SUBLANES, LANES), F32),
        ],
        compiler_params=_params("arbitrary", "arbitrary"),
        name="lru_mixer",
    )(h.reshape(bsz, seq, d), g.reshape(1, d), w_in, conv_w, conv_b.reshape(1, d), w_a, b_a.reshape(1, d),
      w_x, b_x.reshape(1, d), lam.reshape(1, d), w_out)
    return out.reshape(t, d)


def _qkv_kernel(x_ref, g_ref, w_ref, o_ref):
    d = x_ref.shape[-1]
    xn = _rms_norm(x_ref[0], g_ref[...]).astype(BF16)
    qkv = _dot(xn, w_ref[...])
    o_ref[0, :, :d] = (qkv[:, :d] * SB_HEAD_DIM ** -0.5).astype(BF16)
    o_ref[0, :, d:] = qkv[:, d:].astype(BF16)


def _qkv_proj(h, g, w_qkv, bsz):
    t, d = h.shape
    seq = t // bsz
    tm = PROJ_TM
    return pl.pallas_call(
        _qkv_kernel,
        grid=(bsz, seq // tm),
        in_specs=[
            pl.BlockSpec((1, tm, d), lambda b, i: (b, i, 0)),
            _resident((1, d)),
            _resident(w_qkv.shape),
        ],
        out_specs=pl.BlockSpec((1, tm, 3 * d), lambda b, i: (b, i, 0)),
        out_shape=jax.ShapeDtypeStruct((bsz, seq, 3 * d), BF16),
        compiler_params=_params("parallel", "parallel"),
        name="sb_qkv",
    )(h.reshape(bsz, seq, d), g.reshape(1, d), w_qkv)


def _neg_abs(x):
    bits = lax.bitcast_convert_type(x, jnp.uint32) | jnp.uint32(0x80000000)
    return lax.bitcast_convert_type(bits, F32)


def _att_kernel(q_ref, k_ref, v_ref, tri_ref, o_ref, qpad_ref, acc_ref, carry_ref, *, tq, tk, n_pairs):
    i = pl.program_id(2)
    hd = SB_HEAD_DIM
    causal = lax.broadcasted_iota(jnp.int32, (tq, tk), 1) < lax.broadcasted_iota(jnp.int32, (tq, tk), 0)
    low = lax.broadcasted_iota(jnp.int32, (tq, LANES), 1) < hd
    for p in range(n_pairs):
        qp = q_ref[0, :, p * LANES:(p + 1) * LANES]
        zero = jnp.zeros_like(qp)
        qpad_ref[2 * p] = jnp.where(low, qp, zero)
        qpad_ref[2 * p + 1] = jnp.where(low, zero, qp)
    acc_ref[...] = jnp.zeros(acc_ref.shape, F32)
    carry_ref[...] = jnp.zeros(carry_ref.shape, F32)

    chains = range(2 * n_pairs)

    def block(s0, masked):
        def scores(c):
            kb = k_ref[0, pl.ds(s0, tk), (c // 2) * LANES:(c // 2 + 1) * LANES]
            return lax.dot_general(qpad_ref[c], kb, (((1,), (1,)), ((), ())), preferred_element_type=F32)

        def keep_logs(z):
            softplus = jnp.maximum(z, 0.0) + jnp.log(1.0 + jnp.exp(_neg_abs(z)))
            sp = jnp.where(causal, softplus, 0.0) if masked else softplus
            sp_hi = sp.astype(BF16)
            sp_lo = (sp - sp_hi.astype(F32)).astype(BF16)
            right = _dot(sp_hi, tri_ref[...]) + _dot(sp_lo, tri_ref[...])
            return z - softplus, sp, right

        def weigh_values(c, logsig, sp, right):
            carry = carry_ref[c]
            carry_wide = jnp.concatenate([carry] * (tk // LANES), axis=1)
            w = jnp.exp((logsig - carry_wide) - right)
            if masked:
                w = jnp.where(causal, w, 0.0)
            carry_ref[c] = carry + jnp.broadcast_to(right[:, 0:1] + sp[:, 0:1], carry.shape)
            vb = v_ref[0, pl.ds(s0, tk), (c // 2) * LANES:(c // 2 + 1) * LANES]
            acc_ref[c] += _dot(w.astype(BF16), vb)

        z_next = scores(0)
        pending = None
        for c in chains:
            z = z_next
            if c + 1 < len(chains):
                z_next = scores(c + 1)
            stage = keep_logs(z)
            if pending is not None:
                weigh_values(c - 1, *pending)
            pending = stage
        weigh_values(len(chains) - 1, *pending)

    block(pl.multiple_of(i * tk, tk), True)

    def body(n, _):
        block(pl.multiple_of((i - 1 - n) * tk, tk), False)
        return 0

    lax.fori_loop(0, i, body, 0)
    outs = [jnp.where(low, acc_ref[2 * p], acc_ref[2 * p + 1]) for p in range(n_pairs)]
    o_ref[0] = jnp.concatenate(outs, axis=1).astype(o_ref.dtype)


def _attention(qkv, d):
    bsz, seq, _ = qkv.shape
    tq, tk = ATT_TQ, ATT_TK
    assert tq == tk
    n_groups = d // ATT_LANES
    n_pairs = ATT_LANES // LANES
    tri = (lax.broadcasted_iota(jnp.int32, (tk, tk), 0) > lax.broadcasted_iota(jnp.int32, (tk, tk), 1)).astype(BF16)
    return pl.pallas_call(
        functools.partial(_att_kernel, tq=tq, tk=tk, n_pairs=n_pairs),
        grid=(bsz, n_groups, seq // tq),
        in_specs=[
            pl.BlockSpec((1, tq, ATT_LANES), lambda b, g, i: (b, i, g)),
            pl.BlockSpec((1, seq, ATT_LANES), lambda b, g, i: (b, 0, n_groups + g)),
            pl.BlockSpec((1, seq, ATT_LANES), lambda b, g, i: (b, 0, 2 * n_groups + g)),
            _resident((tk, tk)),
        ],
        out_specs=pl.BlockSpec((1, tq, ATT_LANES), lambda b, g, i: (b, i, g)),
        out_shape=jax.ShapeDtypeStruct((bsz, seq, d), BF16),
        scratch_shapes=[
            pltpu.VMEM((2 * n_pairs, tq, LANES), BF16),
            pltpu.VMEM((2 * n_pairs, tq, LANES), F32),
            pltpu.VMEM((2 * n_pairs, tq, LANES), F32),
        ],
        compiler_params=_params("parallel", "parallel", "arbitrary"),
        name="sb_attention",
    )(qkv, qkv, qkv, tri)


def _proj_kernel(a_ref, w_ref, x_ref, o_ref):
    o_ref[...] = x_ref[...] + _dot(a_ref[...], w_ref[...])


def _out_proj(a, w, h):
    t, d = h.shape
    tm = PROJ_TM
    return pl.pallas_call(
        _proj_kernel,
        grid=(t // tm,),
        in_specs=[
            pl.BlockSpec((tm, a.shape[1]), lambda i: (i, 0)),
            _resident(w.shape),
            pl.BlockSpec((tm, d), lambda i: (i, 0)),
        ],
        out_specs=pl.BlockSpec((tm, d), lambda i: (i, 0)),
        out_shape=jax.ShapeDtypeStruct((t, d), F32),
        compiler_params=_params("parallel"),
        name="sb_out_proj",
    )(a, w, h)


def _sb_mixer(h, g, w_qkv, w_out, bsz):
    t, d = h.shape
    qkv = _qkv_proj(h, g, w_qkv, bsz)
    o = _attention(qkv, d)
    return _out_proj(o.reshape(t, d), w_out, h)


def kernel(x, ffn1_norm, ffn1_w_in, ffn1_w_out, mix_norm, ffn2_norm, ffn2_w_in, ffn2_w_out, final_norm,
           s5_w_in, s5_lam_re, s5_lam_im, s5_log_dt, s5_b_re, s5_b_im, s5_c_re, s5_c_im, s5_d, s5_w_out,
           sb_w_qkv, sb_w_out,
           lru_w_in, lru_conv_w, lru_conv_b, lru_w_a, lru_b_a, lru_w_x, lru_b_x, lru_lambda, lru_w_out):
    bsz, seq, d = x.shape
    depth = ffn1_norm.shape[0]
    h = x.reshape(bsz * seq, d)
    bf = lambda w: w.astype(BF16)
    for layer in range(depth):
        h = _ffn(h, ffn1_norm[layer], bf(ffn1_w_in[layer]), bf(ffn1_w_out[layer]))
        kind, j = layer % N_MIXERS, layer // N_MIXERS
        if kind == 0:
            h = _s5_mixer(h, mix_norm[layer], bf(s5_w_in[j]), s5_lam_re[j], s5_lam_im[j], s5_log_dt[j],
                          s5_b_re[j], s5_b_im[j], s5_c_re[j], s5_c_im[j], s5_d[j], bf(s5_w_out[j]), bsz)
        elif kind == 1:
            h = _sb_mixer(h, mix_norm[layer], bf(sb_w_qkv[j]), bf(sb_w_out[j]), bsz)
        else:
            h = _lru_mixer(h, mix_norm[layer], bf(lru_w_in[j]), lru_conv_w[j], lru_conv_b[j], bf(lru_w_a[j]),
                           lru_b_a[j], bf(lru_w_x[j]), lru_b_x[j], lru_lambda[j], bf(lru_w_out[j]), bsz)
        h = _ffn(h, ffn2_norm[layer], bf(ffn2_w_in[layer]), bf(ffn2_w_out[layer]))
    return _final_norm(h, final_norm).reshape(bsz, seq, d)
```

```python
import functools
import math

import jax
import jax.numpy as jnp
from jax import lax
from jax.experimental import pallas as pl
from jax.experimental.pallas import tpu as pltpu

F32 = jnp.float32
BF16 = jnp.bfloat16

RMS_EPS = 1e-6
N_MIXERS = 3
S5_GROUP = 16
S5_STATE = 64
SB_HEAD_DIM = 64
LRU_BLOCK_WIDTH = 256
LRU_CONV = 4
LRU_C = 8.0
MASKED_LOG = -1e30

LANES = 128
SUBLANES = 8
VMEM_LIMIT_BYTES = 56 * 1024 * 1024

FFN_TM = 512
FFN_CHUNK = 256
S5_TM = 256
LRU_TM = 256
PROJ_TM = 512
ATT_TQ = 256
ATT_TK = 256
ATT_LANES = 512


def _params(*sem):
    return pltpu.CompilerParams(dimension_semantics=sem, vmem_limit_bytes=VMEM_LIMIT_BYTES)


def _resident(shape):
    nd = len(shape)
    return pl.BlockSpec(shape, lambda *_: (0,) * nd, pipeline_mode=pl.Buffered(1))


def _rms_norm(x, g):
    ms = jnp.mean(x * x, axis=-1, keepdims=True)
    return (x * lax.rsqrt(ms + RMS_EPS)) * g


def _gelu_tanh(x):
    c = math.sqrt(2.0 / math.pi)
    return 0.5 * x * (1.0 + jnp.tanh(c * (x + 0.044715 * (x * x * x))))


def _dot(a, b):
    return jnp.dot(a, b, preferred_element_type=F32)


def _ffn_kernel(x_ref, g_ref, win_ref, wout_ref, o_ref, *, d_ff, chunk):
    x = x_ref[...]
    xn = _rms_norm(x, g_ref[...]).astype(BF16)
    acc = jnp.zeros(x.shape, F32)
    for c in range(d_ff // chunk):
        gate = _dot(xn, win_ref[:, c * chunk:(c + 1) * chunk])
        up = _dot(xn, win_ref[:, d_ff + c * chunk:d_ff + (c + 1) * chunk])
        act = (gate * jax.nn.sigmoid(gate) * up).astype(BF16)
        acc = acc + _dot(act, wout_ref[c * chunk:(c + 1) * chunk, :])
    o_ref[...] = x + 0.5 * acc


def _ffn(h, g, w_in, w_out):
    t, d = h.shape
    d_ff = w_out.shape[0]
    tm = FFN_TM
    return pl.pallas_call(
        functools.partial(_ffn_kernel, d_ff=d_ff, chunk=FFN_CHUNK),
        grid=(t // tm,),
        in_specs=[
            pl.BlockSpec((tm, d), lambda i: (i, 0)),
            _resident((1, d)),
            _resident((d, 2 * d_ff)),
            _resident((d_ff, d)),
        ],
        out_specs=pl.BlockSpec((tm, d), lambda i: (i, 0)),
        out_shape=jax.ShapeDtypeStruct((t, d), F32),
        compiler_params=_params("parallel"),
        name="ffn",
    )(h, g.reshape(1, d), w_in, w_out)


def _norm_kernel(x_ref, g_ref, o_ref):
    o_ref[...] = _rms_norm(x_ref[...], g_ref[...])


def _final_norm(h, g):
    t, d = h.shape
    tm = PROJ_TM
    return pl.pallas_call(
        _norm_kernel,
        grid=(t // tm,),
        in_specs=[pl.BlockSpec((tm, d), lambda i: (i, 0)), _resident((1, d))],
        out_specs=pl.BlockSpec((tm, d), lambda i: (i, 0)),
        out_shape=jax.ShapeDtypeStruct((t, d), F32),
        compiler_params=_params("parallel"),
        name="final_norm",
    )(h, g.reshape(1, d))


def _s5_kernel(x_ref, g_ref, win_ref, bsmall_ref, csmall_ref, are_ref, aim_ref, dskip_ref, wout_ref,
               o_ref, bmat_ref, cmat_ref, xre_ref, xim_ref, sre_ref, sim_ref, *, tm, n_pairs):
    d = x_ref.shape[-1]
    blk = 2 * LANES

    @pl.when((pl.program_id(0) == 0) & (pl.program_id(1) == 0))
    def _():
        slot_of_row = (lax.broadcasted_iota(jnp.int32, (blk, blk), 0) // S5_GROUP) % SUBLANES
        slot_of_col = (lax.broadcasted_iota(jnp.int32, (blk, blk), 1) // S5_GROUP) % SUBLANES
        for o in range(n_pairs):
            b_o = bsmall_ref[o]
            c_o = csmall_ref[o]
            for j in range(SUBLANES):
                bmat_ref[o, :, j * blk:(j + 1) * blk] = jnp.where(slot_of_row == j, b_o, jnp.zeros_like(b_o))
                cmat_ref[o, j * blk:(j + 1) * blk, :] = jnp.where(slot_of_col == j, c_o, jnp.zeros_like(c_o))

    @pl.when(pl.program_id(1) == 0)
    def _():
        sre_ref[...] = jnp.zeros(sre_ref.shape, F32)
        sim_ref[...] = jnp.zeros(sim_ref.shape, F32)

    x = x_ref[0]
    hn = _rms_norm(x, g_ref[...]).astype(BF16)
    u = _dot(hn, win_ref[...])
    ub = u.astype(BF16)

    for o in range(n_pairs):
        lhs = jnp.concatenate(
            [ub[:, LANES * o:LANES * (o + 1)], ub[:, LANES * (o + n_pairs):LANES * (o + n_pairs + 1)]], axis=1)
        bu = _dot(lhs, bmat_ref[o])
        for j in range(SUBLANES):
            xre_ref[o, pl.ds(j, tm, stride=SUBLANES), :] = bu[:, 2 * LANES * j:2 * LANES * j + LANES]
            xim_ref[o, pl.ds(j, tm, stride=SUBLANES), :] = bu[:, 2 * LANES * j + LANES:2 * LANES * (j + 1)]

    a_re = [are_ref[o] for o in range(n_pairs)]
    a_im = [aim_ref[o] for o in range(n_pairs)]

    def step(t, carry):
        row = pl.multiple_of(t * SUBLANES, SUBLANES)
        new = []
        for o in range(n_pairs):
            s_re, s_im = carry[2 * o], carry[2 * o + 1]
            n_re = a_re[o] * s_re - a_im[o] * s_im + xre_ref[o, pl.ds(row, SUBLANES), :]
            n_im = a_re[o] * s_im + a_im[o] * s_re + xim_ref[o, pl.ds(row, SUBLANES), :]
            xre_ref[o, pl.ds(row, SUBLANES), :] = n_re
            xim_ref[o, pl.ds(row, SUBLANES), :] = n_im
            new += [n_re, n_im]
        return tuple(new)

    init = []
    for o in range(n_pairs):
        init += [sre_ref[o], sim_ref[o]]
    final = lax.fori_loop(0, tm, step, tuple(init), unroll=4)
    for o in range(n_pairs):
        sre_ref[o] = final[2 * o]
        sim_ref[o] = final[2 * o + 1]

    y_tiles = [None] * (2 * n_pairs)
    for o in range(n_pairs):
        pieces = []
        for j in range(SUBLANES):
            pieces.append(xre_ref[o, pl.ds(j, tm, stride=SUBLANES), :].astype(BF16))
            pieces.append(xim_ref[o, pl.ds(j, tm, stride=SUBLANES), :].astype(BF16))
        y_o = _dot(jnp.concatenate(pieces, axis=1), cmat_ref[o])
        y_tiles[o] = y_o[:, :LANES]
        y_tiles[o + n_pairs] = y_o[:, LANES:]
    y = jnp.concatenate(y_tiles, axis=1) + dskip_ref[...] * u
    z = _gelu_tanh(y).astype(BF16)
    vg = _dot(z, wout_ref[...])
    o_ref[0] = x + vg[:, :d] * jax.nn.sigmoid(vg[:, d:])


def _s5_operands(lam_re, lam_im, log_dt, b_re, b_im, c_re, c_im):
    g, p = lam_re.shape
    hch = b_re.shape[-1]
    n_pairs = g // (2 * SUBLANES)
    lam = lax.complex(lam_re.astype(F32), lam_im.astype(F32))
    lam_dt = lam * jnp.exp(log_dt.astype(F32))[:, None]
    lam_bar = jnp.exp(lam_dt)
    b_bar = ((lam_bar - 1.0) / lam)[:, :, None] * lax.complex(b_re.astype(F32), b_im.astype(F32))
    eye_a = jnp.eye(2, dtype=F32)
    bparts = jnp.stack([b_bar.real, b_bar.imag]).reshape(2, 2, n_pairs, SUBLANES, p, hch)
    b_small = jnp.einsum('qaojph,ab->oajhqbp', bparts, eye_a).reshape(n_pairs, 2 * SUBLANES * hch, 2 * 2 * p)
    cparts = jnp.stack([c_re.astype(F32), -c_im.astype(F32)]).reshape(2, 2, n_pairs, SUBLANES, hch, p)
    c_small = jnp.einsum('qaojhp,ab->oqapbjh', cparts, eye_a).reshape(n_pairs, 2 * 2 * p, 2 * SUBLANES * hch)

    def lanes(v):
        return v.reshape(2, n_pairs, SUBLANES, p).transpose(1, 2, 0, 3).reshape(n_pairs, SUBLANES, 2 * p)

    return b_small.astype(BF16), c_small.astype(BF16), lanes(lam_bar.real), lanes(lam_bar.imag)


def _s5_mixer(h, g, w_in, lam_re, lam_im, log_dt, b_re, b_im, c_re, c_im, d_skip, w_out, bsz):
    t, d = h.shape
    seq = t // bsz
    tm = S5_TM
    b_small, c_small, a_re, a_im = _s5_operands(lam_re, lam_im, log_dt, b_re, b_im, c_re, c_im)
    n_pairs = b_small.shape[0]
    blk = 2 * LANES
    assert b_small.shape[1:] == (blk, blk) and c_small.shape[1:] == (blk, blk)
    assert 2 * n_pairs * LANES == d and 2 * S5_STATE == LANES
    out = pl.pallas_call(
        functools.partial(_s5_kernel, tm=tm, n_pairs=n_pairs),
        grid=(bsz, seq // tm),
        in_specs=[
            pl.BlockSpec((1, tm, d), lambda b, i: (b, i, 0)),
            _resident((1, d)),
            _resident(w_in.shape),
            _resident(b_small.shape),
            _resident(c_small.shape),
            _resident(a_re.shape),
            _resident(a_im.shape),
            _resident((1, d)),
            _resident(w_out.shape),
        ],
        out_specs=pl.BlockSpec((1, tm, d), lambda b, i: (b, i, 0)),
        out_shape=jax.ShapeDtypeStruct((bsz, seq, d), F32),
        scratch_shapes=[
            pltpu.VMEM((n_pairs, blk, SUBLANES * blk), BF16),
            pltpu.VMEM((n_pairs, SUBLANES * blk, blk), BF16),
            pltpu.VMEM((n_pairs, tm * SUBLANES, LANES), F32),
            pltpu.VMEM((n_pairs, tm * SUBLANES, LANES), F32),
            pltpu.VMEM((n_pairs, SUBLANES, LANES), F32),
            pltpu.VMEM((n_pairs, SUBLANES, LANES), F32),
        ],
        compiler_params=_params("arbitrary", "arbitrary"),
        name="s5_mixer",
    )(h.reshape(bsz, seq, d), g.reshape(1, d), w_in, b_small, c_small, a_re, a_im, d_skip.reshape(1, d), w_out)
    return out.reshape(t, d)


def _lru_kernel(x_ref, g_ref, win_ref, cw_ref, cb_ref, wa_ref, ba_ref, wx_ref, bx_ref, lam_ref, wout_ref,
                o_ref, br_ref, a_ref, gx_ref, h_ref, *, tm, n_blocks):
    d = x_ref.shape[-1]
    n_tiles = d // LANES
    first = pl.program_id(1) == 0

    @pl.when(first)
    def _():
        br_ref[0:SUBLANES, :] = jnp.zeros((SUBLANES, d), F32)
        h_ref[...] = jnp.zeros(h_ref.shape, F32)

    x = x_ref[0]
    hn = _rms_norm(x, g_ref[...]).astype(BF16)
    both = _dot(hn, win_ref[...])
    branch_gelu = both[:, :d]
    br_ref[SUBLANES:SUBLANES + tm, :] = both[:, d:]
    xc = cb_ref[...]
    for k in range(LRU_CONV):
        off = SUBLANES - (LRU_CONV - 1) + k
        xc = xc + cw_ref[k:k + 1, :] * br_ref[off:off + tm, :]
    br_ref[0:SUBLANES, :] = br_ref[tm:tm + SUBLANES, :]

    xcb = xc.astype(BF16)
    bw = LRU_BLOCK_WIDTH
    r_parts, i_parts = [], []
    for n in range(n_blocks):
        xs = xcb[:, n * bw:(n + 1) * bw]
        r_parts.append(jax.nn.sigmoid(_dot(xs, wa_ref[n]) + ba_ref[:, n * bw:(n + 1) * bw]))
        i_parts.append(jax.nn.sigmoid(_dot(xs, wx_ref[n]) + bx_ref[:, n * bw:(n + 1) * bw]))
    r = jnp.concatenate(r_parts, axis=1)
    gate_i = jnp.concatenate(i_parts, axis=1)
    lam = lam_ref[...]
    softplus_neg_lam = jnp.maximum(-lam, 0.0) + jnp.log1p(jnp.exp(-jnp.abs(lam)))
    log_a = (-LRU_C * r) * softplus_neg_lam
    a = jnp.exp(log_a)
    gated = (gate_i * xc) * jnp.sqrt(-jnp.tanh(log_a) * (a * a + 1.0))

    for c in range(n_tiles):
        a_ref[pl.ds(c, tm, stride=SUBLANES), :] = a[:, c * LANES:(c + 1) * LANES]
        gx_ref[pl.ds(c, tm, stride=SUBLANES), :] = gated[:, c * LANES:(c + 1) * LANES]

    def step(t, hprev):
        row = pl.multiple_of(t * SUBLANES, SUBLANES)
        hnew = a_ref[pl.ds(row, SUBLANES), :] * hprev + gx_ref[pl.ds(row, SUBLANES), :]
        gx_ref[pl.ds(row, SUBLANES), :] = hnew
        return hnew

    h_ref[...] = lax.fori_loop(0, tm, step, h_ref[...], unroll=8)
    hseq = jnp.concatenate([gx_ref[pl.ds(c, tm, stride=SUBLANES), :] for c in range(n_tiles)], axis=1)
    y = (_gelu_tanh(branch_gelu) * hseq).astype(BF16)
    o_ref[0] = x + _dot(y, wout_ref[...])


def _lru_mixer(h, g, w_in, conv_w, conv_b, w_a, b_a, w_x, b_x, lam, w_out, bsz):
    t, d = h.shape
    seq = t // bsz
    tm = LRU_TM
    n_blocks = w_a.shape[0]
    assert d == SUBLANES * LANES and n_blocks * LRU_BLOCK_WIDTH == d
    out = pl.pallas_call(
        functools.partial(_lru_kernel, tm=tm, n_blocks=n_blocks),
        grid=(bsz, seq // tm),
        in_specs=[
            pl.BlockSpec((1, tm, d), lambda b, i: (b, i, 0)),
            _resident((1, d)),
            _resident(w_in.shape),
            _resident(conv_w.shape),
            _resident((1, d)),
            _resident(w_a.shape),
            _resident((1, d)),
            _resident(w_x.shape),
            _resident((1, d)),
            _resident((1, d)),
            _resident(w_out.shape),
        ],
        out_specs=pl.BlockSpec((1, tm, d), lambda b, i: (b, i, 0)),
        out_shape=jax.ShapeDtypeStruct((bsz, seq, d), F32),
        scratch_shapes=[
            pltpu.VMEM((tm + SUBLANES, d), F32),
            pltpu.VMEM((tm * SUBLANES, LANES), F32),
            pltpu.VMEM((tm * SUBLANES, LANES), F32),
            pltpu.VMEM((SUBLANES, LANES), F32),
        ],
        compiler_params=_params("arbitrary", "arbitrary"),
        name="lru_mixer",
    )(h.reshape(bsz, seq, d), g.reshape(1, d), w_in, conv_w, conv_b.reshape(1, d), w_a, b_a.reshape(1, d),
      w_x, b_x.reshape(1, d), lam.reshape(1, d), w_out)
    return out.reshape(t, d)


def _qkv_kernel(x_ref, g_ref, w_ref, o_ref):
    d = x_ref.shape[-1]
    xn = _rms_norm(x_ref[0], g_ref[...]).astype(BF16)
    qkv = _dot(xn, w_ref[...])
    o_ref[0, :, :d] = (qkv[:, :d] * SB_HEAD_DIM ** -0.5).astype(BF16)
    o_ref[0, :, d:] = qkv[:, d:].astype(BF16)


def _qkv_proj(h, g, w_qkv, bsz):
    t, d = h.shape
    seq = t // bsz
    tm = PROJ_TM
    return pl.pallas_call(
        _qkv_kernel,
        grid=(bsz, seq // tm),
        in_specs=[
            pl.BlockSpec((1, tm, d), lambda b, i: (b, i, 0)),
            _resident((1, d)),
            _resident(w_qkv.shape),
        ],
        out_specs=pl.BlockSpec((1, tm, 3 * d), lambda b, i: (b, i, 0)),
        out_shape=jax.ShapeDtypeStruct((bsz, seq, 3 * d), BF16),
        compiler_params=_params("parallel", "parallel"),
        name="sb_qkv",
    )(h.reshape(bsz, seq, d), g.reshape(1, d), w_qkv)


def _neg_abs(x):
    bits = lax.bitcast_convert_type(x, jnp.uint32) | jnp.uint32(0x80000000)
    return lax.bitcast_convert_type(bits, F32)


def _att_kernel(q_ref, k_ref, v_ref, tri_ref, o_ref, qpad_ref, acc_ref, carry_ref, z_ref,
                psp_ref, plog_ref, *, tq, tk, n_pairs):
    i = pl.program_id(2)
    hd = SB_HEAD_DIM
    n_ch = 2 * n_pairs
    causal = lax.broadcasted_iota(jnp.int32, (tq, tk), 1) < lax.broadcasted_iota(jnp.int32, (tq, tk), 0)
    low = lax.broadcasted_iota(jnp.int32, (tq, LANES), 1) < hd
    for p in range(n_pairs):
        qp = q_ref[0, :, p * LANES:(p + 1) * LANES]
        zero = jnp.zeros_like(qp)
        qpad_ref[2 * p] = jnp.where(low, qp, zero)
        qpad_ref[2 * p + 1] = jnp.where(low, zero, qp)
    acc_ref[...] = jnp.zeros(acc_ref.shape, F32)
    carry_ref[...] = jnp.zeros(carry_ref.shape, F32)

    def head_lanes(c):
        return slice((c // 2) * LANES, (c // 2 + 1) * LANES)

    def scores(c, s0):
        kb = k_ref[0, pl.ds(s0, tk), head_lanes(c)]
        return lax.dot_general(qpad_ref[c], kb, (((1,), (1,)), ((), ())), preferred_element_type=F32)

    def keep_logs(z, masked):
        softplus = jnp.maximum(z, 0.0) + jnp.log(1.0 + jnp.exp(_neg_abs(z)))
        logsig = z - softplus
        if masked:
            sp = jnp.where(causal, softplus, 0.0).astype(BF16)
            logsig = jnp.where(causal, logsig, MASKED_LOG)
        else:
            sp = softplus.astype(BF16)
        return logsig, sp

    def suffix_sums(sp):
        return _dot(sp, tri_ref[...])

    def weigh_values(c, s0, logsig, sp_first, right):
        carry = carry_ref[c]
        carry_wide = jnp.concatenate([carry] * (tk // LANES), axis=1)
        w = jnp.exp((logsig - carry_wide) - right)
        carry_ref[c] = carry + jnp.broadcast_to(right[:, 0:1] + sp_first[:, 0:1].astype(F32), carry.shape)
        vb = v_ref[0, pl.ds(s0, tk), head_lanes(c)]
        acc_ref[c] += _dot(w.astype(BF16), vb)

    def finish_last_head(s0):
        right = suffix_sums(psp_ref[...])
        return lambda: weigh_values(n_ch - 1, s0, plog_ref[...], psp_ref[:, :LANES], right)

    def run_block(s0, masked, s0_unfinished):
        s0_next = pl.multiple_of(jnp.maximum(s0 - tk, 0), tk)
        finish = finish_last_head(s0_unfinished) if s0_unfinished is not None else None
        ahead = {}
        previous = None
        for c in range(n_ch):
            nxt = c + 2
            ahead[nxt] = scores(nxt, s0) if nxt < n_ch else scores(nxt - n_ch, s0_next)
            z = z_ref[c] if c < 2 else ahead.pop(c)
            logsig, sp = keep_logs(z, masked)
            if c < n_ch - 1:
                stage = (logsig, sp[:, :LANES], suffix_sums(sp))
            else:
                plog_ref[...] = logsig
                psp_ref[...] = sp
            if c == 0:
                if finish is not None:
                    finish()
            else:
                weigh_values(c - 1, s0, *previous)
            previous = stage
        z_ref[0] = ahead[n_ch]
        z_ref[1] = ahead[n_ch + 1]

    s_diag = pl.multiple_of(i * tk, tk)
    z_ref[0] = scores(0, s_diag)
    z_ref[1] = scores(1, s_diag)
    run_block(s_diag, True, None)

    def body(n, _):
        s0 = pl.multiple_of((i - 1 - n) * tk, tk)
        run_block(s0, False, pl.multiple_of(s0 + tk, tk))
        return 0

    lax.fori_loop(0, i, body, 0)
    finish_last_head(0)()
    outs = [jnp.where(low, acc_ref[2 * p], acc_ref[2 * p + 1]) for p in range(n_pairs)]
    o_ref[0] = jnp.concatenate(outs, axis=1).astype(o_ref.dtype)


def _attention(qkv, d):
    bsz, seq, _ = qkv.shape
    tq, tk = ATT_TQ, ATT_TK
    assert tq == tk
    n_groups = d // ATT_LANES
    n_pairs = ATT_LANES // LANES
    tri = (lax.broadcasted_iota(jnp.int32, (tk, tk), 0) > lax.broadcasted_iota(jnp.int32, (tk, tk), 1)).astype(BF16)
    return pl.pallas_call(
        functools.partial(_att_kernel, tq=tq, tk=tk, n_pairs=n_pairs),
        grid=(bsz, n_groups, seq // tq),
        in_specs=[
            pl.BlockSpec((1, tq, ATT_LANES), lambda b, g, i: (b, i, g)),
            pl.BlockSpec((1, seq, ATT_LANES), lambda b, g, i: (b, 0, n_groups + g)),
            pl.BlockSpec((1, seq, ATT_LANES), lambda b, g, i: (b, 0, 2 * n_groups + g)),
            _resident((tk, tk)),
        ],
        out_specs=pl.BlockSpec((1, tq, ATT_LANES), lambda b, g, i: (b, i, g)),
        out_shape=jax.ShapeDtypeStruct((bsz, seq, d), BF16),
        scratch_shapes=[
            pltpu.VMEM((2 * n_pairs, tq, LANES), BF16),
            pltpu.VMEM((2 * n_pairs, tq, LANES), F32),
            pltpu.VMEM((2 * n_pairs, tq, LANES), F32),
            pltpu.VMEM((2, tq, tk), F32),
            pltpu.VMEM((tq, tk), BF16),
            pltpu.VMEM((tq, tk), F32),
        ],
        compiler_params=_params("parallel", "parallel", "arbitrary"),
        name="sb_attention",
    )(qkv, qkv, qkv, tri)


def _proj_kernel(a_ref, w_ref, x_ref, o_ref):
    o_ref[...] = x_ref[...] + _dot(a_ref[...], w_ref[...])


def _out_proj(a, w, h):
    t, d = h.shape
    tm = PROJ_TM
    return pl.pallas_call(
        _proj_kernel,
        grid=(t // tm,),
        in_specs=[
            pl.BlockSpec((tm, a.shape[1]), lambda i: (i, 0)),
            _resident(w.shape),
            pl.BlockSpec((tm, d), lambda i: (i, 0)),
        ],
        out_specs=pl.BlockSpec((tm, d), lambda i: (i, 0)),
        out_shape=jax.ShapeDtypeStruct((t, d), F32),
        compiler_params=_params("parallel"),
        name="sb_out_proj",
    )(a, w, h)


def _sb_mixer(h, g, w_qkv, w_out, bsz):
    t, d = h.shape
    qkv = _qkv_proj(h, g, w_qkv, bsz)
    o = _attention(qkv, d)
    return _out_proj(o.reshape(t, d), w_out, h)


def kernel(x, ffn1_norm, ffn1_w_in, ffn1_w_out, mix_norm, ffn2_norm, ffn2_w_in, ffn2_w_out, final_norm,
           s5_w_in, s5_lam_re, s5_lam_im, s5_log_dt, s5_b_re, s5_b_im, s5_c_re, s5_c_im, s5_d, s5_w_out,
           sb_w_qkv, sb_w_out,
           lru_w_in, lru_conv_w, lru_conv_b, lru_w_a, lru_b_a, lru_w_x, lru_b_x, lru_lambda, lru_w_out):
    bsz, seq, d = x.shape
    depth = ffn1_norm.shape[0]
    h = x.reshape(bsz * seq, d)
    bf = lambda w: w.astype(BF16)
    for layer in range(depth):
        h = _ffn(h, ffn1_norm[layer], bf(ffn1_w_in[layer]), bf(ffn1_w_out[layer]))
        kind, j = layer % N_MIXERS, layer // N_MIXERS
        if kind == 0:
            h = _s5_mixer(h, mix_norm[layer], bf(s5_w_in[j]), s5_lam_re[j], s5_lam_im[j], s5_log_dt[j],
                          s5_b_re[j], s5_b_im[j], s5_c_re[j], s5_c_im[j], s5_d[j], bf(s5_w_out[j]), bsz)
        elif kind == 1:
            h = _sb_mixer(h, mix_norm[layer], bf(sb_w_qkv[j]), bf(sb_w_out[j]), bsz)
        else:
            h = _lru_mixer(h, mix_norm[layer], bf(lru_w_in[j]), lru_conv_w[j], lru_conv_b[j], bf(lru_w_a[j]),
                           lru_b_a[j], bf(lru_w_x[j]), lru_b_x[j], lru_lambda[j], bf(lru_w_out[j]), bsz)
        h = _ffn(h, ffn2_norm[layer], bf(ffn2_w_in[layer]), bf(ffn2_w_out[layer]))
    return _final_norm(h, final_norm).reshape(bsz, seq, d)
```

```python
import functools
import math

import jax
import jax.numpy as jnp
from jax import lax
from jax.experimental import pallas as pl
from jax.experimental.pallas import tpu as pltpu

F32 = jnp.float32
BF16 = jnp.bfloat16

RMS_EPS = 1e-6
N_MIXERS = 3
S5_GROUP = 16
S5_STATE = 64
SB_HEAD_DIM = 64
LRU_BLOCK_WIDTH = 256
LRU_CONV = 4
LRU_C = 8.0
MASKED_LOG = -1e30

LANES = 128
SUBLANES = 8
VMEM_LIMIT_BYTES = 56 * 1024 * 1024

FFN_TM = 512
FFN_CHUNK = 256
S5_TM = 256
LRU_TM = 256
PROJ_TM = 512
ATT_TQ = 256
ATT_TK = 256
ATT_LANES = 512


def _params(*sem):
    return pltpu.CompilerParams(dimension_semantics=sem, vmem_limit_bytes=VMEM_LIMIT_BYTES)


def _resident(shape):
    nd = len(shape)
    return pl.BlockSpec(shape, lambda *_: (0,) * nd, pipeline_mode=pl.Buffered(1))


def _rms_norm(x, g):
    ms = jnp.mean(x * x, axis=-1, keepdims=True)
    return (x * lax.rsqrt(ms + RMS_EPS)) * g


def _gelu_tanh(x):
    c = math.sqrt(2.0 / math.pi)
    return 0.5 * x * (1.0 + jnp.tanh(c * (x + 0.044715 * (x * x * x))))


def _dot(a, b):
    return jnp.dot(a, b, preferred_element_type=F32)


def _ffn_kernel(x_ref, g_ref, win_ref, wout_ref, *rest, d_ff, chunk):
    o_ref = rest[-1]
    x = x_ref[...]
    xn = _rms_norm(x, g_ref[...]).astype(BF16)
    acc = jnp.zeros(x.shape, F32)
    for c in range(d_ff // chunk):
        gate = _dot(xn, win_ref[:, c * chunk:(c + 1) * chunk])
        up = _dot(xn, win_ref[:, d_ff + c * chunk:d_ff + (c + 1) * chunk])
        act = (gate * jax.nn.sigmoid(gate) * up).astype(BF16)
        acc = acc + _dot(act, wout_ref[c * chunk:(c + 1) * chunk, :])
    y = x + 0.5 * acc
    if len(rest) == 2:
        y = _rms_norm(y, rest[0][...])
    o_ref[...] = y


def _ffn(h, g, w_in_all, w_out_all, layer, final_g=None):
    t, d = h.shape
    d_ff = w_out_all.shape[1]
    tm = FFN_TM

    def layer_weights(rows, cols):
        return pl.BlockSpec((None, rows, cols), lambda i: (layer, 0, 0), pipeline_mode=pl.Buffered(1))

    in_specs = [
        pl.BlockSpec((tm, d), lambda i: (i, 0)),
        _resident((1, d)),
        layer_weights(d, 2 * d_ff),
        layer_weights(d_ff, d),
    ]
    operands = [h, g.reshape(1, d), w_in_all, w_out_all]
    if final_g is not None:
        in_specs.append(_resident((1, d)))
        operands.append(final_g.reshape(1, d))
    return pl.pallas_call(
        functools.partial(_ffn_kernel, d_ff=d_ff, chunk=FFN_CHUNK),
        grid=(t // tm,),
        in_specs=in_specs,
        out_specs=pl.BlockSpec((tm, d), lambda i: (i, 0)),
        out_shape=jax.ShapeDtypeStruct((t, d), F32),
        compiler_params=_params("parallel"),
        name="ffn",
    )(*operands)


def _s5_kernel(x_ref, g_ref, win_ref, bsmall_ref, csmall_ref, are_ref, aim_ref, dskip_ref, wout_ref,
               o_ref, bmat_ref, cmat_ref, xre_ref, xim_ref, sre_ref, sim_ref, *, tm, n_pairs):
    d = x_ref.shape[-1]
    blk = 2 * LANES

    @pl.when((pl.program_id(0) == 0) & (pl.program_id(1) == 0))
    def _():
        slot_of_row = (lax.broadcasted_iota(jnp.int32, (blk, blk), 0) // S5_GROUP) % SUBLANES
        slot_of_col = (lax.broadcasted_iota(jnp.int32, (blk, blk), 1) // S5_GROUP) % SUBLANES
        for o in range(n_pairs):
            b_o = bsmall_ref[o]
            c_o = csmall_ref[o]
            for j in range(SUBLANES):
                bmat_ref[o, :, j * blk:(j + 1) * blk] = jnp.where(slot_of_row == j, b_o, jnp.zeros_like(b_o))
                cmat_ref[o, j * blk:(j + 1) * blk, :] = jnp.where(slot_of_col == j, c_o, jnp.zeros_like(c_o))

    @pl.when(pl.program_id(1) == 0)
    def _():
        sre_ref[...] = jnp.zeros(sre_ref.shape, F32)
        sim_ref[...] = jnp.zeros(sim_ref.shape, F32)

    x = x_ref[0]
    hn = _rms_norm(x, g_ref[...]).astype(BF16)
    u = _dot(hn, win_ref[...])
    ub = u.astype(BF16)

    for o in range(n_pairs):
        lhs = jnp.concatenate(
            [ub[:, LANES * o:LANES * (o + 1)], ub[:, LANES * (o + n_pairs):LANES * (o + n_pairs + 1)]], axis=1)
        bu = _dot(lhs, bmat_ref[o])
        for j in range(SUBLANES):
            xre_ref[o, pl.ds(j, tm, stride=SUBLANES), :] = bu[:, 2 * LANES * j:2 * LANES * j + LANES]
            xim_ref[o, pl.ds(j, tm, stride=SUBLANES), :] = bu[:, 2 * LANES * j + LANES:2 * LANES * (j + 1)]

    a_re = [are_ref[o] for o in range(n_pairs)]
    a_im = [aim_ref[o] for o in range(n_pairs)]

    def step(t, carry):
        row = pl.multiple_of(t * SUBLANES, SUBLANES)
        new = []
        for o in range(n_pairs):
            s_re, s_im = carry[2 * o], carry[2 * o + 1]
            n_re = a_re[o] * s_re - a_im[o] * s_im + xre_ref[o, pl.ds(row, SUBLANES), :]
            n_im = a_re[o] * s_im + a_im[o] * s_re + xim_ref[o, pl.ds(row, SUBLANES), :]
            xre_ref[o, pl.ds(row, SUBLANES), :] = n_re
            xim_ref[o, pl.ds(row, SUBLANES), :] = n_im
            new += [n_re, n_im]
        return tuple(new)

    init = []
    for o in range(n_pairs):
        init += [sre_ref[o], sim_ref[o]]
    final = lax.fori_loop(0, tm, step, tuple(init), unroll=4)
    for o in range(n_pairs):
        sre_ref[o] = final[2 * o]
        sim_ref[o] = final[2 * o + 1]

    y_tiles = [None] * (2 * n_pairs)
    for o in range(n_pairs):
        pieces = []
        for j in range(SUBLANES):
            pieces.append(xre_ref[o, pl.ds(j, tm, stride=SUBLANES), :].astype(BF16))
            pieces.append(xim_ref[o, pl.ds(j, tm, stride=SUBLANES), :].astype(BF16))
        y_o = _dot(jnp.concatenate(pieces, axis=1), cmat_ref[o])
        y_tiles[o] = y_o[:, :LANES]
        y_tiles[o + n_pairs] = y_o[:, LANES:]
    y = jnp.concatenate(y_tiles, axis=1) + dskip_ref[...] * u
    z = _gelu_tanh(y).astype(BF16)
    vg = _dot(z, wout_ref[...])
    o_ref[0] = x + vg[:, :d] * jax.nn.sigmoid(vg[:, d:])


def _s5_operands(lam_re, lam_im, log_dt, b_re, b_im, c_re, c_im):
    g, p = lam_re.shape
    hch = b_re.shape[-1]
    n_pairs = g // (2 * SUBLANES)
    lam = lax.complex(lam_re.astype(F32), lam_im.astype(F32))
    lam_dt = lam * jnp.exp(log_dt.astype(F32))[:, None]
    lam_bar = jnp.exp(lam_dt)
    b_bar = ((lam_bar - 1.0) / lam)[:, :, None] * lax.complex(b_re.astype(F32), b_im.astype(F32))
    eye_a = jnp.eye(2, dtype=F32)
    bparts = jnp.stack([b_bar.real, b_bar.imag]).reshape(2, 2, n_pairs, SUBLANES, p, hch)
    b_small = jnp.einsum('qaojph,ab->oajhqbp', bparts, eye_a).reshape(n_pairs, 2 * SUBLANES * hch, 2 * 2 * p)
    cparts = jnp.stack([c_re.astype(F32), -c_im.astype(F32)]).reshape(2, 2, n_pairs, SUBLANES, hch, p)
    c_small = jnp.einsum('qaojhp,ab->oqapbjh', cparts, eye_a).reshape(n_pairs, 2 * 2 * p, 2 * SUBLANES * hch)

    def lanes(v):
        return v.reshape(2, n_pairs, SUBLANES, p).transpose(1, 2, 0, 3).reshape(n_pairs, SUBLANES, 2 * p)

    return b_small.astype(BF16), c_small.astype(BF16), lanes(lam_bar.real), lanes(lam_bar.imag)


def _s5_mixer(h, g, w_in, lam_re, lam_im, log_dt, b_re, b_im, c_re, c_im, d_skip, w_out, bsz):
    t, d = h.shape
    seq = t // bsz
    tm = S5_TM
    b_small, c_small, a_re, a_im = _s5_operands(lam_re, lam_im, log_dt, b_re, b_im, c_re, c_im)
    n_pairs = b_small.shape[0]
    blk = 2 * LANES
    assert b_small.shape[1:] == (blk, blk) and c_small.shape[1:] == (blk, blk)
    assert 2 * n_pairs * LANES == d and 2 * S5_STATE == LANES
    out = pl.pallas_call(
        functools.partial(_s5_kernel, tm=tm, n_pairs=n_pairs),
        grid=(bsz, seq // tm),
        in_specs=[
            pl.BlockSpec((1, tm, d), lambda b, i: (b, i, 0)),
            _resident((1, d)),
            _resident(w_in.shape),
            _resident(b_small.shape),
            _resident(c_small.shape),
            _resident(a_re.shape),
            _resident(a_im.shape),
            _resident((1, d)),
            _resident(w_out.shape),
        ],
        out_specs=pl.BlockSpec((1, tm, d), lambda b, i: (b, i, 0)),
        out_shape=jax.ShapeDtypeStruct((bsz, seq, d), F32),
        scratch_shapes=[
            pltpu.VMEM((n_pairs, blk, SUBLANES * blk), BF16),
            pltpu.VMEM((n_pairs, SUBLANES * blk, blk), BF16),
            pltpu.VMEM((n_pairs, tm * SUBLANES, LANES), F32),
            pltpu.VMEM((n_pairs, tm * SUBLANES, LANES), F32),
            pltpu.VMEM((n_pairs, SUBLANES, LANES), F32),
            pltpu.VMEM((n_pairs, SUBLANES, LANES), F32),
        ],
        compiler_params=_params("arbitrary", "arbitrary"),
        name="s5_mixer",
    )(h.reshape(bsz, seq, d), g.reshape(1, d), w_in, b_small, c_small, a_re, a_im, d_skip.reshape(1, d), w_out)
    return out.reshape(t, d)


def _lru_kernel(x_ref, g_ref, win_ref, cw_ref, cb_ref, wa_ref, ba_ref, wx_ref, bx_ref, lam_ref, wout_ref,
                o_ref, br_ref, a_ref, gx_ref, h_ref, *, tm, n_blocks):
    d = x_ref.shape[-1]
    n_tiles = d // LANES
    first = pl.program_id(1) == 0

    @pl.when(first)
    def _():
        br_ref[0:SUBLANES, :] = jnp.zeros((SUBLANES, d), F32)
        h_ref[...] = jnp.zeros(h_ref.shape, F32)

    x = x_ref[0]
    hn = _rms_norm(x, g_ref[...]).astype(BF16)
    both = _dot(hn, win_ref[...])
    branch_gelu = both[:, :d]
    br_ref[SUBLANES:SUBLANES + tm, :] = both[:, d:]
    xc = cb_ref[...]
    for k in range(LRU_CONV):
        off = SUBLANES - (LRU_CONV - 1) + k
        xc = xc + cw_ref[k:k + 1, :] * br_ref[off:off + tm, :]
    br_ref[0:SUBLANES, :] = br_ref[tm:tm + SUBLANES, :]

    xcb = xc.astype(BF16)
    bw = LRU_BLOCK_WIDTH
    r_parts, i_parts = [], []
    for n in range(n_blocks):
        xs = xcb[:, n * bw:(n + 1) * bw]
        r_parts.append(jax.nn.sigmoid(_dot(xs, wa_ref[n]) + ba_ref[:, n * bw:(n + 1) * bw]))
        i_parts.append(jax.nn.sigmoid(_dot(xs, wx_ref[n]) + bx_ref[:, n * bw:(n + 1) * bw]))
    r = jnp.concatenate(r_parts, axis=1)
    gate_i = jnp.concatenate(i_parts, axis=1)
    lam = lam_ref[...]
    softplus_neg_lam = jnp.maximum(-lam, 0.0) + jnp.log1p(jnp.exp(-jnp.abs(lam)))
    log_a = (-LRU_C * r) * softplus_neg_lam
    a = jnp.exp(log_a)
    gated = (gate_i * xc) * jnp.sqrt(-jnp.tanh(log_a) * (a * a + 1.0))

    for c in range(n_tiles):
        a_ref[pl.ds(c, tm, stride=SUBLANES), :] = a[:, c * LANES:(c + 1) * LANES]
        gx_ref[pl.ds(c, tm, stride=SUBLANES), :] = gated[:, c * LANES:(c + 1) * LANES]

    def step(t, hprev):
        row = pl.multiple_of(t * SUBLANES, SUBLANES)
        hnew = a_ref[pl.ds(row, SUBLANES), :] * hprev + gx_ref[pl.ds(row, SUBLANES), :]
        gx_ref[pl.ds(row, SUBLANES), :] = hnew
        return hnew

    h_ref[...] = lax.fori_loop(0, tm, step, h_ref[...], unroll=8)
    hseq = jnp.concatenate([gx_ref[pl.ds(c, tm, stride=SUBLANES), :] for c in range(n_tiles)], axis=1)
    y = (_gelu_tanh(branch_gelu) * hseq).astype(BF16)
    o_ref[0] = x + _dot(y, wout_ref[...])


def _lru_mixer(h, g, w_in, conv_w, conv_b, w_a, b_a, w_x, b_x, lam, w_out, bsz):
    t, d = h.shape
    seq = t // bsz
    tm = LRU_TM
    n_blocks = w_a.shape[0]
    assert d == SUBLANES * LANES and n_blocks * LRU_BLOCK_WIDTH == d
    out = pl.pallas_call(
        functools.partial(_lru_kernel, tm=tm, n_blocks=n_blocks),
        grid=(bsz, seq // tm),
        in_specs=[
            pl.BlockSpec((1, tm, d), lambda b, i: (b, i, 0)),
            _resident((1, d)),
            _resident(w_in.shape),
            _resident(conv_w.shape),
            _resident((1, d)),
            _resident(w_a.shape),
            _resident((1, d)),
            _resident(w_x.shape),
            _resident((1, d)),
            _resident((1, d)),
            _resident(w_out.shape),
        ],
        out_specs=pl.BlockSpec((1, tm, d), lambda b, i: (b, i, 0)),
        out_shape=jax.ShapeDtypeStruct((bsz, seq, d), F32),
        scratch_shapes=[
            pltpu.VMEM((tm + SUBLANES, d), F32),
            pltpu.VMEM((tm * SUBLANES, LANES), F32),
            pltpu.VMEM((tm * SUBLANES, LANES), F32),
            pltpu.VMEM((SUBLANES, LANES), F32),
        ],
        compiler_params=_params("arbitrary", "arbitrary"),
        name="lru_mixer",
    )(h.reshape(bsz, seq, d), g.reshape(1, d), w_in, conv_w, conv_b.reshape(1, d), w_a, b_a.reshape(1, d),
      w_x, b_x.reshape(1, d), lam.reshape(1, d), w_out)
    return out.reshape(t, d)


def _qkv_kernel(x_ref, g_ref, w_ref, o_ref):
    d = x_ref.shape[-1]
    xn = _rms_norm(x_ref[0], g_ref[...]).astype(BF16)
    qkv = _dot(xn, w_ref[...])
    o_ref[0, :, :d] = (qkv[:, :d] * SB_HEAD_DIM ** -0.5).astype(BF16)
    o_ref[0, :, d:] = qkv[:, d:].astype(BF16)


def _qkv_proj(h, g, w_qkv, bsz):
    t, d = h.shape
    seq = t // bsz
    tm = PROJ_TM
    return pl.pallas_call(
        _qkv_kernel,
        grid=(bsz, seq // tm),
        in_specs=[
            pl.BlockSpec((1, tm, d), lambda b, i: (b, i, 0)),
            _resident((1, d)),
            _resident(w_qkv.shape),
        ],
        out_specs=pl.BlockSpec((1, tm, 3 * d), lambda b, i: (b, i, 0)),
        out_shape=jax.ShapeDtypeStruct((bsz, seq, 3 * d), BF16),
        compiler_params=_params("parallel", "parallel"),
        name="sb_qkv",
    )(h.reshape(bsz, seq, d), g.reshape(1, d), w_qkv)


def _neg_abs(x):
    bits = lax.bitcast_convert_type(x, jnp.uint32) | jnp.uint32(0x80000000)
    return lax.bitcast_convert_type(bits, F32)


def _att_kernel(q_ref, k_ref, v_ref, tri_ref, o_ref, qpad_ref, acc_ref, carry_ref, z_ref,
                psp_ref, plog_ref, *, tq, tk, n_pairs):
    i = pl.program_id(2)
    hd = SB_HEAD_DIM
    n_ch = 2 * n_pairs
    causal = lax.broadcasted_iota(jnp.int32, (tq, tk), 1) < lax.broadcasted_iota(jnp.int32, (tq, tk), 0)
    low = lax.broadcasted_iota(jnp.int32, (tq, LANES), 1) < hd
    for p in range(n_pairs):
        qp = q_ref[0, :, p * LANES:(p + 1) * LANES]
        zero = jnp.zeros_like(qp)
        qpad_ref[2 * p] = jnp.where(low, qp, zero)
        qpad_ref[2 * p + 1] = jnp.where(low, zero, qp)
    acc_ref[...] = jnp.zeros(acc_ref.shape, F32)
    carry_ref[...] = jnp.zeros(carry_ref.shape, F32)

    def head_lanes(c):
        return slice((c // 2) * LANES, (c // 2 + 1) * LANES)

    def scores(c, s0):
        kb = k_ref[0, pl.ds(s0, tk), head_lanes(c)]
        return lax.dot_general(qpad_ref[c], kb, (((1,), (1,)), ((), ())), preferred_element_type=F32)

    def keep_logs(z, masked):
        softplus = jnp.maximum(z, 0.0) + jnp.log(1.0 + jnp.exp(_neg_abs(z)))
        logsig = z - softplus
        if masked:
            sp = jnp.where(causal, softplus, 0.0).astype(BF16)
            logsig = jnp.where(causal, logsig, MASKED_LOG)
        else:
            sp = softplus.astype(BF16)
        return logsig, sp

    def suffix_sums(sp):
        return _dot(sp, tri_ref[...])

    def weigh_values(c, s0, logsig, sp_first, right):
        carry = carry_ref[c]
        carry_wide = jnp.concatenate([carry] * (tk // LANES), axis=1)
        w = jnp.exp((logsig - carry_wide) - right)
        carry_ref[c] = carry + jnp.broadcast_to(right[:, 0:1] + sp_first[:, 0:1].astype(F32), carry.shape)
        vb = v_ref[0, pl.ds(s0, tk), head_lanes(c)]
        acc_ref[c] += _dot(w.astype(BF16), vb)

    def finish_last_head(s0):
        right = suffix_sums(psp_ref[...])
        return lambda: weigh_values(n_ch - 1, s0, plog_ref[...], psp_ref[:, :LANES], right)

    def run_block(s0, masked, s0_unfinished):
        s0_next = pl.multiple_of(jnp.maximum(s0 - tk, 0), tk)
        finish = finish_last_head(s0_unfinished) if s0_unfinished is not None else None
        ahead = {}
        previous = None
        for c in range(n_ch):
            nxt = c + 2
            ahead[nxt] = scores(nxt, s0) if nxt < n_ch else scores(nxt - n_ch, s0_next)
            z = z_ref[c] if c < 2 else ahead.pop(c)
            logsig, sp = keep_logs(z, masked)
            if c < n_ch - 1:
                stage = (logsig, sp[:, :LANES], suffix_sums(sp))
            else:
                plog_ref[...] = logsig
                psp_ref[...] = sp
            if c == 0:
                if finish is not None:
                    finish()
            else:
                weigh_values(c - 1, s0, *previous)
            previous = stage
        z_ref[0] = ahead[n_ch]
        z_ref[1] = ahead[n_ch + 1]

    s_diag = pl.multiple_of(i * tk, tk)
    z_ref[0] = scores(0, s_diag)
    z_ref[1] = scores(1, s_diag)
    run_block(s_diag, True, None)

    def body(n, _):
        s0 = pl.multiple_of((i - 1 - n) * tk, tk)
        run_block(s0, False, pl.multiple_of(s0 + tk, tk))
        return 0

    lax.fori_loop(0, i, body, 0)
    finish_last_head(0)()
    outs = [jnp.where(low, acc_ref[2 * p], acc_ref[2 * p + 1]) for p in range(n_pairs)]
    o_ref[0] = jnp.concatenate(outs, axis=1).astype(o_ref.dtype)


def _attention(qkv, d):
    bsz, seq, _ = qkv.shape
    tq, tk = ATT_TQ, ATT_TK
    assert tq == tk
    n_groups = d // ATT_LANES
    n_pairs = ATT_LANES // LANES
    tri = (lax.broadcasted_iota(jnp.int32, (tk, tk), 0) > lax.broadcasted_iota(jnp.int32, (tk, tk), 1)).astype(BF16)
    return pl.pallas_call(
        functools.partial(_att_kernel, tq=tq, tk=tk, n_pairs=n_pairs),
        grid=(bsz, n_groups, seq // tq),
        in_specs=[
            pl.BlockSpec((1, tq, ATT_LANES), lambda b, g, i: (b, i, g)),
            pl.BlockSpec((1, seq, ATT_LANES), lambda b, g, i: (b, 0, n_groups + g)),
            pl.BlockSpec((1, seq, ATT_LANES), lambda b, g, i: (b, 0, 2 * n_groups + g)),
            _resident((tk, tk)),
        ],
        out_specs=pl.BlockSpec((1, tq, ATT_LANES), lambda b, g, i: (b, i, g)),
        out_shape=jax.ShapeDtypeStruct((bsz, seq, d), BF16),
        scratch_shapes=[
            pltpu.VMEM((2 * n_pairs, tq, LANES), BF16),
            pltpu.VMEM((2 * n_pairs, tq, LANES), F32),
            pltpu.VMEM((2 * n_pairs, tq, LANES), F32),
            pltpu.VMEM((2, tq, tk), F32),
            pltpu.VMEM((tq, tk), BF16),
            pltpu.VMEM((tq, tk), F32),
        ],
        compiler_params=_params("parallel", "parallel", "arbitrary"),
        name="sb_attention",
    )(qkv, qkv, qkv, tri)


def _proj_kernel(a_ref, w_ref, x_ref, o_ref):
    o_ref[...] = x_ref[...] + _dot(a_ref[...], w_ref[...])


def _out_proj(a, w, h):
    t, d = h.shape
    tm = PROJ_TM
    return pl.pallas_call(
        _proj_kernel,
        grid=(t // tm,),
        in_specs=[
            pl.BlockSpec((tm, a.shape[1]), lambda i: (i, 0)),
            _resident(w.shape),
            pl.BlockSpec((tm, d), lambda i: (i, 0)),
        ],
        out_specs=pl.BlockSpec((tm, d), lambda i: (i, 0)),
        out_shape=jax.ShapeDtypeStruct((t, d), F32),
        compiler_params=_params("parallel"),
        name="sb_out_proj",
    )(a, w, h)


def _sb_mixer(h, g, w_qkv, w_out, bsz):
    t, d = h.shape
    qkv = _qkv_proj(h, g, w_qkv, bsz)
    o = _attention(qkv, d)
    return _out_proj(o.reshape(t, d), w_out, h)


def kernel(x, ffn1_norm, ffn1_w_in, ffn1_w_out, mix_norm, ffn2_norm, ffn2_w_in, ffn2_w_out, final_norm,
           s5_w_in, s5_lam_re, s5_lam_im, s5_log_dt, s5_b_re, s5_b_im, s5_c_re, s5_c_im, s5_d, s5_w_out,
           sb_w_qkv, sb_w_out,
           lru_w_in, lru_conv_w, lru_conv_b, lru_w_a, lru_b_a, lru_w_x, lru_b_x, lru_lambda, lru_w_out):
    bsz, seq, d = x.shape
    depth = ffn1_norm.shape[0]
    h = x.reshape(bsz * seq, d)
    bf = lambda w: w.astype(BF16)
    w1_in, w1_out, w2_in, w2_out = bf(ffn1_w_in), bf(ffn1_w_out), bf(ffn2_w_in), bf(ffn2_w_out)
    for layer in range(depth):
        h = _ffn(h, ffn1_norm[layer], w1_in, w1_out, layer)
        kind, j = layer % N_MIXERS, layer // N_MIXERS
        if kind == 0:
            h = _s5_mixer(h, mix_norm[layer], bf(s5_w_in[j]), s5_lam_re[j], s5_lam_im[j], s5_log_dt[j],
                          s5_b_re[j], s5_b_im[j], s5_c_re[j], s5_c_im[j], s5_d[j], bf(s5_w_out[j]), bsz)
        elif kind == 1:
            h = _sb_mixer(h, mix_norm[layer], bf(sb_w_qkv[j]), bf(sb_w_out[j]), bsz)
        else:
            h = _lru_mixer(h, mix_norm[layer], bf(lru_w_in[j]), lru_conv_w[j], lru_conv_b[j], bf(lru_w_a[j]),
                           lru_b_a[j], bf(lru_w_x[j]), lru_b_x[j], lru_lambda[j], bf(lru_w_out[j]), bsz)
        h = _ffn(h, ffn2_norm[layer], w2_in, w2_out, layer, final_norm if layer == depth - 1 else None)
    return h.reshape(bsz, seq, d)
```

```python
import functools
import math

import jax
import jax.numpy as jnp
from jax import lax
from jax.experimental import pallas as pl
from jax.experimental.pallas import tpu as pltpu

F32 = jnp.float32
BF16 = jnp.bfloat16

RMS_EPS = 1e-6
N_MIXERS = 3
S5_GROUP = 16
S5_STATE = 64
SB_HEAD_DIM = 64
LRU_BLOCK_WIDTH = 256
LRU_CONV = 4
LRU_C = 8.0
MASKED_LOG = -1e30

LANES = 128
SUBLANES = 8
VMEM_LIMIT_BYTES = 56 * 1024 * 1024

FFN_TM = 512
FFN_CHUNK = 256
S5_TM = 256
LRU_TM = 512
PROJ_TM = 512
ATT_TQ = 256
ATT_TK = 256
ATT_LANES = 512


def _params(*sem):
    return pltpu.CompilerParams(dimension_semantics=sem, vmem_limit_bytes=VMEM_LIMIT_BYTES)


def _resident(shape):
    nd = len(shape)
    return pl.BlockSpec(shape, lambda *_: (0,) * nd, pipeline_mode=pl.Buffered(1))


def _rms_norm(x, g):
    ms = jnp.mean(x * x, axis=-1, keepdims=True)
    return (x * lax.rsqrt(ms + RMS_EPS)) * g


def _gelu_tanh(x):
    c = math.sqrt(2.0 / math.pi)
    return 0.5 * x * (1.0 + jnp.tanh(c * (x + 0.044715 * (x * x * x))))


def _dot(a, b):
    return jnp.dot(a, b, preferred_element_type=F32)


def _ffn_kernel(x_ref, g_ref, win_ref, wout_ref, *rest, d_ff, chunk):
    o_ref = rest[-1]
    x = x_ref[...]
    xn = _rms_norm(x, g_ref[...]).astype(BF16)
    acc = jnp.zeros(x.shape, F32)
    for c in range(d_ff // chunk):
        gate = _dot(xn, win_ref[:, c * chunk:(c + 1) * chunk])
        up = _dot(xn, win_ref[:, d_ff + c * chunk:d_ff + (c + 1) * chunk])
        act = (gate * jax.nn.sigmoid(gate) * up).astype(BF16)
        acc = acc + _dot(act, wout_ref[c * chunk:(c + 1) * chunk, :])
    y = x + 0.5 * acc
    if len(rest) == 2:
        y = _rms_norm(y, rest[0][...])
    o_ref[...] = y


def _ffn(h, g, w_in_all, w_out_all, layer, final_g=None):
    t, d = h.shape
    d_ff = w_out_all.shape[1]
    tm = FFN_TM

    def layer_weights(rows, cols):
        return pl.BlockSpec((None, rows, cols), lambda i: (layer, 0, 0), pipeline_mode=pl.Buffered(1))

    in_specs = [
        pl.BlockSpec((tm, d), lambda i: (i, 0)),
        _resident((1, d)),
        layer_weights(d, 2 * d_ff),
        layer_weights(d_ff, d),
    ]
    operands = [h, g.reshape(1, d), w_in_all, w_out_all]
    if final_g is not None:
        in_specs.append(_resident((1, d)))
        operands.append(final_g.reshape(1, d))
    return pl.pallas_call(
        functools.partial(_ffn_kernel, d_ff=d_ff, chunk=FFN_CHUNK),
        grid=(t // tm,),
        in_specs=in_specs,
        out_specs=pl.BlockSpec((tm, d), lambda i: (i, 0)),
        out_shape=jax.ShapeDtypeStruct((t, d), F32),
        compiler_params=_params("parallel"),
        name="ffn",
    )(*operands)


def _s5_kernel(x_ref, g_ref, win_ref, bsmall_ref, csmall_ref, are_ref, aim_ref, dskip_ref, wout_ref,
               o_ref, bmat_ref, cmat_ref, xre_ref, xim_ref, sre_ref, sim_ref, *, tm, n_pairs):
    d = x_ref.shape[-1]
    blk = 2 * LANES

    @pl.when((pl.program_id(0) == 0) & (pl.program_id(1) == 0))
    def _():
        slot_of_row = (lax.broadcasted_iota(jnp.int32, (blk, blk), 0) // S5_GROUP) % SUBLANES
        slot_of_col = (lax.broadcasted_iota(jnp.int32, (blk, blk), 1) // S5_GROUP) % SUBLANES
        for o in range(n_pairs):
            b_o = bsmall_ref[o]
            c_o = csmall_ref[o]
            for j in range(SUBLANES):
                bmat_ref[o, :, j * blk:(j + 1) * blk] = jnp.where(slot_of_row == j, b_o, jnp.zeros_like(b_o))
                cmat_ref[o, j * blk:(j + 1) * blk, :] = jnp.where(slot_of_col == j, c_o, jnp.zeros_like(c_o))

    @pl.when(pl.program_id(1) == 0)
    def _():
        sre_ref[...] = jnp.zeros(sre_ref.shape, F32)
        sim_ref[...] = jnp.zeros(sim_ref.shape, F32)

    x = x_ref[0]
    hn = _rms_norm(x, g_ref[...]).astype(BF16)
    u = _dot(hn, win_ref[...])
    ub = u.astype(BF16)

    for o in range(n_pairs):
        lhs = jnp.concatenate(
            [ub[:, LANES * o:LANES * (o + 1)], ub[:, LANES * (o + n_pairs):LANES * (o + n_pairs + 1)]], axis=1)
        bu = _dot(lhs, bmat_ref[o])
        for j in range(SUBLANES):
            xre_ref[o, pl.ds(j, tm, stride=SUBLANES), :] = bu[:, 2 * LANES * j:2 * LANES * j + LANES]
            xim_ref[o, pl.ds(j, tm, stride=SUBLANES), :] = bu[:, 2 * LANES * j + LANES:2 * LANES * (j + 1)]

    a_re = [are_ref[o] for o in range(n_pairs)]
    a_im = [aim_ref[o] for o in range(n_pairs)]

    def step(t, carry):
        row = pl.multiple_of(t * SUBLANES, SUBLANES)
        new = []
        for o in range(n_pairs):
            s_re, s_im = carry[2 * o], carry[2 * o + 1]
            n_re = a_re[o] * s_re - a_im[o] * s_im + xre_ref[o, pl.ds(row, SUBLANES), :]
            n_im = a_re[o] * s_im + a_im[o] * s_re + xim_ref[o, pl.ds(row, SUBLANES), :]
            xre_ref[o, pl.ds(row, SUBLANES), :] = n_re
            xim_ref[o, pl.ds(row, SUBLANES), :] = n_im
            new += [n_re, n_im]
        return tuple(new)

    init = []
    for o in range(n_pairs):
        init += [sre_ref[o], sim_ref[o]]
    final = lax.fori_loop(0, tm, step, tuple(init), unroll=4)
    for o in range(n_pairs):
        sre_ref[o] = final[2 * o]
        sim_ref[o] = final[2 * o + 1]

    y_tiles = [None] * (2 * n_pairs)
    for o in range(n_pairs):
        pieces = []
        for j in range(SUBLANES):
            pieces.append(xre_ref[o, pl.ds(j, tm, stride=SUBLANES), :].astype(BF16))
            pieces.append(xim_ref[o, pl.ds(j, tm, stride=SUBLANES), :].astype(BF16))
        y_o = _dot(jnp.concatenate(pieces, axis=1), cmat_ref[o])
        y_tiles[o] = y_o[:, :LANES]
        y_tiles[o + n_pairs] = y_o[:, LANES:]
    y = jnp.concatenate(y_tiles, axis=1) + dskip_ref[...] * u
    z = _gelu_tanh(y).astype(BF16)
    vg = _dot(z, wout_ref[...])
    o_ref[0] = x + vg[:, :d] * jax.nn.sigmoid(vg[:, d:])


def _s5_operands(lam_re, lam_im, log_dt, b_re, b_im, c_re, c_im):
    g, p = lam_re.shape
    hch = b_re.shape[-1]
    n_pairs = g // (2 * SUBLANES)
    lam = lax.complex(lam_re.astype(F32), lam_im.astype(F32))
    lam_dt = lam * jnp.exp(log_dt.astype(F32))[:, None]
    lam_bar = jnp.exp(lam_dt)
    b_bar = ((lam_bar - 1.0) / lam)[:, :, None] * lax.complex(b_re.astype(F32), b_im.astype(F32))
    eye_a = jnp.eye(2, dtype=F32)
    bparts = jnp.stack([b_bar.real, b_bar.imag]).reshape(2, 2, n_pairs, SUBLANES, p, hch)
    b_small = jnp.einsum('qaojph,ab->oajhqbp', bparts, eye_a).reshape(n_pairs, 2 * SUBLANES * hch, 2 * 2 * p)
    cparts = jnp.stack([c_re.astype(F32), -c_im.astype(F32)]).reshape(2, 2, n_pairs, SUBLANES, hch, p)
    c_small = jnp.einsum('qaojhp,ab->oqapbjh', cparts, eye_a).reshape(n_pairs, 2 * 2 * p, 2 * SUBLANES * hch)

    def lanes(v):
        return v.reshape(2, n_pairs, SUBLANES, p).transpose(1, 2, 0, 3).reshape(n_pairs, SUBLANES, 2 * p)

    return b_small.astype(BF16), c_small.astype(BF16), lanes(lam_bar.real), lanes(lam_bar.imag)


def _s5_mixer(h, g, w_in, lam_re, lam_im, log_dt, b_re, b_im, c_re, c_im, d_skip, w_out, bsz):
    t, d = h.shape
    seq = t // bsz
    tm = S5_TM
    b_small, c_small, a_re, a_im = _s5_operands(lam_re, lam_im, log_dt, b_re, b_im, c_re, c_im)
    n_pairs = b_small.shape[0]
    blk = 2 * LANES
    assert b_small.shape[1:] == (blk, blk) and c_small.shape[1:] == (blk, blk)
    assert 2 * n_pairs * LANES == d and 2 * S5_STATE == LANES
    out = pl.pallas_call(
        functools.partial(_s5_kernel, tm=tm, n_pairs=n_pairs),
        grid=(bsz, seq // tm),
        in_specs=[
            pl.BlockSpec((1, tm, d), lambda b, i: (b, i, 0)),
            _resident((1, d)),
            _resident(w_in.shape),
            _resident(b_small.shape),
            _resident(c_small.shape),
            _resident(a_re.shape),
            _resident(a_im.shape),
            _resident((1, d)),
            _resident(w_out.shape),
        ],
        out_specs=pl.BlockSpec((1, tm, d), lambda b, i: (b, i, 0)),
        out_shape=jax.ShapeDtypeStruct((bsz, seq, d), F32),
        scratch_shapes=[
            pltpu.VMEM((n_pairs, blk, SUBLANES * blk), BF16),
            pltpu.VMEM((n_pairs, SUBLANES * blk, blk), BF16),
            pltpu.VMEM((n_pairs, tm * SUBLANES, LANES), F32),
            pltpu.VMEM((n_pairs, tm * SUBLANES, LANES), F32),
            pltpu.VMEM((n_pairs, SUBLANES, LANES), F32),
            pltpu.VMEM((n_pairs, SUBLANES, LANES), F32),
        ],
        compiler_params=_params("arbitrary", "arbitrary"),
        name="s5_mixer",
    )(h.reshape(bsz, seq, d), g.reshape(1, d), w_in, b_small, c_small, a_re, a_im, d_skip.reshape(1, d), w_out)
    return out.reshape(t, d)


def _lru_kernel(x_ref, g_ref, win_ref, cw_ref, cb_ref, wa_ref, ba_ref, wx_ref, bx_ref, lam_ref, wout_ref,
                o_ref, br_ref, a_ref, gx_ref, h_ref, *, tm, n_blocks):
    d = x_ref.shape[-1]
    n_tiles = d // LANES
    first = pl.program_id(1) == 0

    @pl.when(first)
    def _():
        br_ref[0:SUBLANES, :] = jnp.zeros((SUBLANES, d), F32)
        h_ref[...] = jnp.zeros(h_ref.shape, F32)

    x = x_ref[0]
    hn = _rms_norm(x, g_ref[...]).astype(BF16)
    both = _dot(hn, win_ref[...])
    branch_gelu = both[:, :d]
    br_ref[SUBLANES:SUBLANES + tm, :] = both[:, d:]
    xc = cb_ref[...]
    for k in range(LRU_CONV):
        off = SUBLANES - (LRU_CONV - 1) + k
        xc = xc + cw_ref[k:k + 1, :] * br_ref[off:off + tm, :]
    br_ref[0:SUBLANES, :] = br_ref[tm:tm + SUBLANES, :]

    xcb = xc.astype(BF16)
    bw = LRU_BLOCK_WIDTH
    r_parts, i_parts = [], []
    for n in range(n_blocks):
        xs = xcb[:, n * bw:(n + 1) * bw]
        r_parts.append(jax.nn.sigmoid(_dot(xs, wa_ref[n]) + ba_ref[:, n * bw:(n + 1) * bw]))
        i_parts.append(jax.nn.sigmoid(_dot(xs, wx_ref[n]) + bx_ref[:, n * bw:(n + 1) * bw]))
    r = jnp.concatenate(r_parts, axis=1)
    gate_i = jnp.concatenate(i_parts, axis=1)
    lam = lam_ref[...]
    softplus_neg_lam = jnp.maximum(-lam, 0.0) + jnp.log1p(jnp.exp(-jnp.abs(lam)))
    log_a = (-LRU_C * r) * softplus_neg_lam
    a = jnp.exp(log_a)
    gated = (gate_i * xc) * jnp.sqrt(-jnp.tanh(log_a) * (a * a + 1.0))

    for c in range(n_tiles):
        a_ref[pl.ds(c, tm, stride=SUBLANES), :] = a[:, c * LANES:(c + 1) * LANES]
        gx_ref[pl.ds(c, tm, stride=SUBLANES), :] = gated[:, c * LANES:(c + 1) * LANES]

    def step(t, hprev):
        row = pl.multiple_of(t * SUBLANES, SUBLANES)
        hnew = a_ref[pl.ds(row, SUBLANES), :] * hprev + gx_ref[pl.ds(row, SUBLANES), :]
        gx_ref[pl.ds(row, SUBLANES), :] = hnew
        return hnew

    h_ref[...] = lax.fori_loop(0, tm, step, h_ref[...], unroll=8)
    hseq = jnp.concatenate([gx_ref[pl.ds(c, tm, stride=SUBLANES), :] for c in range(n_tiles)], axis=1)
    y = (_gelu_tanh(branch_gelu) * hseq).astype(BF16)
    o_ref[0] = x + _dot(y, wout_ref[...])


def _lru_mixer(h, g, w_in, conv_w, conv_b, w_a, b_a, w_x, b_x, lam, w_out, bsz):
    t, d = h.shape
    seq = t // bsz
    tm = LRU_TM
    n_blocks = w_a.shape[0]
    assert d == SUBLANES * LANES and n_blocks * LRU_BLOCK_WIDTH == d
    out = pl.pallas_call(
        functools.partial(_lru_kernel, tm=tm, n_blocks=n_blocks),
        grid=(bsz, seq // tm),
        in_specs=[
            pl.BlockSpec((1, tm, d), lambda b, i: (b, i, 0)),
            _resident((1, d)),
            _resident(w_in.shape),
            _resident(conv_w.shape),
            _resident((1, d)),
            _resident(w_a.shape),
            _resident((1, d)),
            _resident(w_x.shape),
            _resident((1, d)),
            _resident((1, d)),
            _resident(w_out.shape),
        ],
        out_specs=pl.BlockSpec((1, tm, d), lambda b, i: (b, i, 0)),
        out_shape=jax.ShapeDtypeStruct((bsz, seq, d), F32),
        scratch_shapes=[
            pltpu.VMEM((tm + SUBLANES, d), F32),
            pltpu.VMEM((tm * SUBLANES, LANES), F32),
            pltpu.VMEM((tm * SUBLANES, LANES), F32),
            pltpu.VMEM((SUBLANES, LANES), F32),
        ],
        compiler_params=_params("arbitrary", "arbitrary"),
        name="lru_mixer",
    )(h.reshape(bsz, seq, d), g.reshape(1, d), w_in, conv_w, conv_b.reshape(1, d), w_a, b_a.reshape(1, d),
      w_x, b_x.reshape(1, d), lam.reshape(1, d), w_out)
    return out.reshape(t, d)


def _qkv_kernel(x_ref, g_ref, w_ref, o_ref):
    d = x_ref.shape[-1]
    xn = _rms_norm(x_ref[0], g_ref[...]).astype(BF16)
    qkv = _dot(xn, w_ref[...])
    o_ref[0, :, :d] = (qkv[:, :d] * SB_HEAD_DIM ** -0.5).astype(BF16)
    o_ref[0, :, d:] = qkv[:, d:].astype(BF16)


def _qkv_proj(h, g, w_qkv, bsz):
    t, d = h.shape
    seq = t // bsz
    tm = PROJ_TM
    return pl.pallas_call(
        _qkv_kernel,
        grid=(bsz, seq // tm),
        in_specs=[
            pl.BlockSpec((1, tm, d), lambda b, i: (b, i, 0)),
            _resident((1, d)),
            _resident(w_qkv.shape),
        ],
        out_specs=pl.BlockSpec((1, tm, 3 * d), lambda b, i: (b, i, 0)),
        out_shape=jax.ShapeDtypeStruct((bsz, seq, 3 * d), BF16),
        compiler_params=_params("parallel", "parallel"),
        name="sb_qkv",
    )(h.reshape(bsz, seq, d), g.reshape(1, d), w_qkv)


def _neg_abs(x):
    bits = lax.bitcast_convert_type(x, jnp.uint32) | jnp.uint32(0x80000000)
    return lax.bitcast_convert_type(bits, F32)


def _att_kernel(q_ref, k_ref, v_ref, tri_ref, o_ref, qpad_ref, acc_ref, carry_ref, z_ref,
                psp_ref, plog_ref, *, tq, tk, n_pairs):
    i = pl.program_id(2)
    hd = SB_HEAD_DIM
    n_ch = 2 * n_pairs
    causal = lax.broadcasted_iota(jnp.int32, (tq, tk), 1) < lax.broadcasted_iota(jnp.int32, (tq, tk), 0)
    low = lax.broadcasted_iota(jnp.int32, (tq, LANES), 1) < hd
    for p in range(n_pairs):
        qp = q_ref[0, :, p * LANES:(p + 1) * LANES]
        zero = jnp.zeros_like(qp)
        qpad_ref[2 * p] = jnp.where(low, qp, zero)
        qpad_ref[2 * p + 1] = jnp.where(low, zero, qp)
    acc_ref[...] = jnp.zeros(acc_ref.shape, F32)
    carry_ref[...] = jnp.zeros(carry_ref.shape, F32)

    def head_lanes(c):
        return slice((c // 2) * LANES, (c // 2 + 1) * LANES)

    def scores(c, s0):
        kb = k_ref[0, pl.ds(s0, tk), head_lanes(c)]
        return lax.dot_general(qpad_ref[c], kb, (((1,), (1,)), ((), ())), preferred_element_type=F32)

    def keep_logs(z, masked):
        softplus = jnp.maximum(z, 0.0) + jnp.log(1.0 + jnp.exp(_neg_abs(z)))
        logsig = z - softplus
        if masked:
            sp = jnp.where(causal, softplus, 0.0).astype(BF16)
            logsig = jnp.where(causal, logsig, MASKED_LOG)
        else:
            sp = softplus.astype(BF16)
        return logsig, sp

    def suffix_sums(sp):
        return _dot(sp, tri_ref[...])

    def weigh_values(c, s0, logsig, sp_first, right):
        carry = carry_ref[c]
        carry_wide = jnp.concatenate([carry] * (tk // LANES), axis=1)
        w = jnp.exp(((logsig - carry_wide) - right).astype(BF16))
        carry_ref[c] = carry + jnp.broadcast_to(right[:, 0:1] + sp_first[:, 0:1].astype(F32), carry.shape)
        vb = v_ref[0, pl.ds(s0, tk), head_lanes(c)]
        acc_ref[c] += _dot(w, vb)

    def finish_last_head(s0):
        right = suffix_sums(psp_ref[...])
        return lambda: weigh_values(n_ch - 1, s0, plog_ref[...], psp_ref[:, :LANES], right)

    def run_block(s0, masked, s0_unfinished):
        s0_next = pl.multiple_of(jnp.maximum(s0 - tk, 0), tk)
        finish = finish_last_head(s0_unfinished) if s0_unfinished is not None else None
        ahead = {}
        previous = None
        for c in range(n_ch):
            nxt = c + 2
            ahead[nxt] = scores(nxt, s0) if nxt < n_ch else scores(nxt - n_ch, s0_next)
            z = z_ref[c] if c < 2 else ahead.pop(c)
            logsig, sp = keep_logs(z, masked)
            if c < n_ch - 1:
                stage = (logsig, sp[:, :LANES], suffix_sums(sp))
            else:
                plog_ref[...] = logsig
                psp_ref[...] = sp
            if c == 0:
                if finish is not None:
                    finish()
            else:
                weigh_values(c - 1, s0, *previous)
            previous = stage
        z_ref[0] = ahead[n_ch]
        z_ref[1] = ahead[n_ch + 1]

    s_diag = pl.multiple_of(i * tk, tk)
    z_ref[0] = scores(0, s_diag)
    z_ref[1] = scores(1, s_diag)
    run_block(s_diag, True, None)

    def body(n, _):
        s0 = pl.multiple_of((i - 1 - n) * tk, tk)
        run_block(s0, False, pl.multiple_of(s0 + tk, tk))
        return 0

    lax.fori_loop(0, i, body, 0)
    finish_last_head(0)()
    outs = [jnp.where(low, acc_ref[2 * p], acc_ref[2 * p + 1]) for p in range(n_pairs)]
    o_ref[0] = jnp.concatenate(outs, axis=1).astype(o_ref.dtype)


def _attention(qkv, d):
    bsz, seq, _ = qkv.shape
    tq, tk = ATT_TQ, ATT_TK
    assert tq == tk
    n_groups = d // ATT_LANES
    n_pairs = ATT_LANES // LANES
    tri = (lax.broadcasted_iota(jnp.int32, (tk, tk), 0) > lax.broadcasted_iota(jnp.int32, (tk, tk), 1)).astype(BF16)
    return pl.pallas_call(
        functools.partial(_att_kernel, tq=tq, tk=tk, n_pairs=n_pairs),
        grid=(bsz, n_groups, seq // tq),
        in_specs=[
            pl.BlockSpec((1, tq, ATT_LANES), lambda b, g, i: (b, i, g)),
            pl.BlockSpec((1, seq, ATT_LANES), lambda b, g, i: (b, 0, n_groups + g)),
            pl.BlockSpec((1, seq, ATT_LANES), lambda b, g, i: (b, 0, 2 * n_groups + g)),
            _resident((tk, tk)),
        ],
        out_specs=pl.BlockSpec((1, tq, ATT_LANES), lambda b, g, i: (b, i, g)),
        out_shape=jax.ShapeDtypeStruct((bsz, seq, d), BF16),
        scratch_shapes=[
            pltpu.VMEM((2 * n_pairs, tq, LANES), BF16),
            pltpu.VMEM((2 * n_pairs, tq, LANES), F32),
            pltpu.VMEM((2 * n_pairs, tq, LANES), F32),
            pltpu.VMEM((2, tq, tk), F32),
            pltpu.VMEM((tq, tk), BF16),
            pltpu.VMEM((tq, tk), F32),
        ],
        compiler_params=_params("parallel", "parallel", "arbitrary"),
        name="sb_attention",
    )(qkv, qkv, qkv, tri)


def _proj_kernel(a_ref, w_ref, x_ref, o_ref):
    o_ref[...] = x_ref[...] + _dot(a_ref[...], w_ref[...])


def _out_proj(a, w, h):
    t, d = h.shape
    tm = PROJ_TM
    return pl.pallas_call(
        _proj_kernel,
        grid=(t // tm,),
        in_specs=[
            pl.BlockSpec((tm, a.shape[1]), lambda i: (i, 0)),
            _resident(w.shape),
            pl.BlockSpec((tm, d), lambda i: (i, 0)),
        ],
        out_specs=pl.BlockSpec((tm, d), lambda i: (i, 0)),
        out_shape=jax.ShapeDtypeStruct((t, d), F32),
        compiler_params=_params("parallel"),
        name="sb_out_proj",
    )(a, w, h)


def _sb_mixer(h, g, w_qkv, w_out, bsz):
    t, d = h.shape
    qkv = _qkv_proj(h, g, w_qkv, bsz)
    o = _attention(qkv, d)
    return _out_proj(o.reshape(t, d), w_out, h)


def kernel(x, ffn1_norm, ffn1_w_in, ffn1_w_out, mix_norm, ffn2_norm, ffn2_w_in, ffn2_w_out, final_norm,
           s5_w_in, s5_lam_re, s5_lam_im, s5_log_dt, s5_b_re, s5_b_im, s5_c_re, s5_c_im, s5_d, s5_w_out,
           sb_w_qkv, sb_w_out,
           lru_w_in, lru_conv_w, lru_conv_b, lru_w_a, lru_b_a, lru_w_x, lru_b_x, lru_lambda, lru_w_out):
    bsz, seq, d = x.shape
    depth = ffn1_norm.shape[0]
    h = x.reshape(bsz * seq, d)
    bf = lambda w: w.astype(BF16)
    w1_in, w1_out, w2_in, w2_out = bf(ffn1_w_in), bf(ffn1_w_out), bf(ffn2_w_in), bf(ffn2_w_out)
    for layer in range(depth):
        h = _ffn(h, ffn1_norm[layer], w1_in, w1_out, layer)
        kind, j = layer % N_MIXERS, layer // N_MIXERS
        if kind == 0:
            h = _s5_mixer(h, mix_norm[layer], bf(s5_w_in[j]), s5_lam_re[j], s5_lam_im[j], s5_log_dt[j],
                          s5_b_re[j], s5_b_im[j], s5_c_re[j], s5_c_im[j], s5_d[j], bf(s5_w_out[j]), bsz)
        elif kind == 1:
            h = _sb_mixer(h, mix_norm[layer], bf(sb_w_qkv[j]), bf(sb_w_out[j]), bsz)
        else:
            h = _lru_mixer(h, mix_norm[layer], bf(lru_w_in[j]), lru_conv_w[j], lru_conv_b[j], bf(lru_w_a[j]),
                           lru_b_a[j], bf(lru_w_x[j]), lru_b_x[j], lru_lambda[j], bf(lru_w_out[j]), bsz)
        h = _ffn(h, ffn2_norm[layer], w2_in, w2_out, layer, final_norm if layer == depth - 1 else None)
    return h.reshape(bsz, seq, d)
```
